```python
import jax, jax.numpy as jnp
from jax import lax
import numpy as np

D_MODEL = 1024
BATCH = 16
SEQ = 2048
DEPTH = 2
DEC_BATCH = 32
DEC_SEQ = 64
PAST_LEN = 4096

CHUNK = 64
N_EVEN = (DEPTH + 1) // 2
N_ODD = DEPTH // 2
PLE_DIM = 256
D_A = D_MODEL // 2
CONV_W = 3
H_F = 8
HD_F = 64
D_F = H_F * HD_F
D_MIX = D_A + D_F
IN_COLS = 3 * D_A + 3 * D_F + H_F
IN_SPLITS = (D_A, 2 * D_A, 3 * D_A, 3 * D_A + D_F, 3 * D_A + 2 * D_F, 3 * D_A + 3 * D_F)
QBLK = 128
HD_R = 64
H_R = D_MODEL // HD_R
LORA_W = 64
LORA_A = 64
LORA_G = 128
D_FF = -(-8 * D_MODEL // (3 * 256)) * 256
NORM_EPS = 1e-6
GN_EPS = 64e-5
L2_EPS = 1e-12
NEG_INF = -1e30
RWKV_NAMES = ('r_mu', 'r_w_r', 'r_w_k', 'r_w_v', 'r_w_o', 'r_w0', 'r_w1', 'r_w2', 'r_a0', 'r_a1', 'r_a2',
              'r_g1', 'r_g2', 'r_k_k', 'r_k_a', 'r_r_k', 'r_ln_w', 'r_ln_b')

kernel_name = 'hybrid_conv_fox_rwkv7_stream_step'


def rms_norm(x, g):
    x32 = x.astype(jnp.float32)
    y = x32 * lax.rsqrt(jnp.mean(x32 * x32, axis=-1, keepdims=True) + NORM_EPS)
    return (y * g.astype(jnp.float32)).astype(x.dtype)


def swiglu(x, w_gate, w_up, w_down):
    return (jax.nn.silu(x @ w_gate) * (x @ w_up)) @ w_down


def short_conv(u, prev, w):
    T = u.shape[1]
    ext = jnp.concatenate([prev.astype(u.dtype), u], axis=1)
    out = sum(ext[:, j:j + T] * w[j] for j in range(CONV_W))
    return out, ext[:, -(CONV_W - 1):]


def fox_prompt(q, k, v, logf):
    B, S = q.shape[:2]
    nb = S // QBLK
    scale = HD_F ** -0.5
    cum = jnp.cumsum(logf, axis=1).transpose(0, 2, 1)
    qb = q.reshape(B, nb, QBLK, H_F, HD_F).transpose(1, 0, 2, 3, 4)
    cq = cum.reshape(B, H_F, nb, QBLK).transpose(2, 0, 1, 3)
    qpos = jnp.arange(S, dtype=jnp.int32).reshape(nb, QBLK)
    kpos = jnp.arange(S, dtype=jnp.int32)

    def block(args):
        qi, ci, pi = args
        s = jnp.einsum('bqhd,bkhd->bhqk', qi, k).astype(jnp.float32) * scale
        s = s + ci[..., None] - cum[:, :, None, :]
        s = jnp.where(pi[:, None] >= kpos[None, :], s, NEG_INF)
        p = jax.nn.softmax(s, axis=-1).astype(v.dtype)
        return jnp.einsum('bhqk,bkhd->bqhd', p, v)

    o = lax.map(block, (qb, cq, qpos))
    return o.transpose(1, 0, 2, 3, 4).reshape(B, S, D_F)


def fox_sample(q, k, v, logf, ck, cv, clogf):
    B, T = q.shape[:2]
    P = ck.shape[1]
    scale = HD_F ** -0.5
    kall = jnp.concatenate([ck.astype(k.dtype), k], axis=1)
    vall = jnp.concatenate([cv.astype(v.dtype), v], axis=1)
    cum = jnp.cumsum(jnp.concatenate([clogf.astype(jnp.float32), logf], axis=1), axis=1).transpose(0, 2, 1)
    s = jnp.einsum('bqhd,bkhd->bhqk', q, kall).astype(jnp.float32) * scale
    s = s + cum[:, :, P:, None] - cum[:, :, None, :]
    qpos = P + jnp.arange(T, dtype=jnp.int32)
    kpos = jnp.arange(P + T, dtype=jnp.int32)
    s = jnp.where(qpos[:, None] >= kpos[None, :], s, NEG_INF)
    p = jax.nn.softmax(s, axis=-1).astype(vall.dtype)
    return jnp.einsum('bhqk,bkhd->bqhd', p, vall).reshape(B, T, D_F)


def conv_fox_mixer(x, conv_prev, fox_cache, w_in, b_f, conv_w, w_out):
    B, T = x.shape[:2]
    z = x @ w_in
    ax, a_b, a_c, q, k, v, fl = jnp.split(z, IN_SPLITS, axis=-1)
    u = a_c * ax
    cu, conv_state = short_conv(u, conv_prev, conv_w)
    y_a = a_b * cu
    logf = jax.nn.log_sigmoid(fl.astype(jnp.float32) + b_f.astype(jnp.float32))
    q = q.reshape(B, T, H_F, HD_F)
    k = k.reshape(B, T, H_F, HD_F)
    v = v.reshape(B, T, H_F, HD_F)
    if fox_cache is None:
        o = fox_prompt(q, k, v, logf)
    else:
        o = fox_sample(q, k, v, logf, *fox_cache)
    y = jnp.concatenate([y_a, o.astype(x.dtype)], axis=-1) @ w_out
    return y, k, v, logf, conv_state


def rwkv7_mixer(x, shift_prev, wkv_prev, mu, w_r, w_k, w_v, w_o, w0, w1, w2, a0, a1, a2,
                g1, g2, k_k, k_a, r_k, ln_w, ln_b):
    f32 = jnp.float32
    B, T, _ = x.shape
    x_prev = jnp.concatenate([shift_prev[:, None, :].astype(x.dtype), x[:, :-1]], axis=1)
    xx = x_prev - x
    xr, xw, xk, xv, xa, xg = (x + xx * mu[n] for n in range(6))
    r = xr @ w_r
    k = xk @ w_k
    v = xv @ w_v
    logw = -jax.nn.softplus(-(w0 + jnp.tanh(xw @ w1) @ w2).astype(f32)) - 0.5
    decay = jnp.exp(-jnp.exp(logw))
    a = jax.nn.sigmoid((a0 + (xa @ a1) @ a2).astype(f32))
    g = jax.nn.sigmoid(xg @ g1) @ g2
    heads = lambda t: t.astype(f32).reshape(B, T, H_R, HD_R)
    kk = heads(k * k_k)
    kk = kk / jnp.maximum(jnp.sqrt(jnp.sum(kk * kk, axis=-1, keepdims=True)), L2_EPS)
    k_mod = k.astype(f32) * (1.0 + (a - 1.0) * k_a.astype(f32))
    r_h, k_h, v_h, w_h, a_h = heads(r), heads(k_mod), heads(v), heads(decay), heads(a)
    b_h = kk * a_h

    def step(S, inp):
        r_t, w_t, k_t, v_t, kk_t, b_t = inp
        sa = jnp.einsum('bhvk,bhk->bhv', S, -kk_t)
        S = S * w_t[:, :, None, :] + sa[..., None] * b_t[:, :, None, :] + v_t[..., None] * k_t[:, :, None, :]
        return S, jnp.einsum('bhvk,bhk->bhv', S, r_t)

    xs = tuple(t.transpose(1, 0, 2, 3) for t in (r_h, w_h, k_h, v_h, kk, b_h))
    S_fin, o = lax.scan(step, wkv_prev.astype(f32), xs)
    o = o.transpose(1, 0, 2, 3)
    mean = jnp.mean(o, axis=-1, keepdims=True)
    var = jnp.mean(jnp.square(o - mean), axis=-1, keepdims=True)
    o = ((o - mean) * lax.rsqrt(var + GN_EPS)).reshape(B, T, D_MODEL)
    o = o * ln_w.astype(f32) + ln_b.astype(f32)
    o = o + (jnp.sum(r_h * k_h * r_k.astype(f32), axis=-1, keepdims=True) * v_h).reshape(B, T, D_MODEL)
    y = (o.astype(x.dtype) * g) @ w_o
    return y, x[:, -1], S_fin


def trunk(x, p, fox_cache, conv_prev, shift_prev, wkv_prev, W):
    h = x
    ks, vs, lfs, convs, shifts, wkvs = [], [], [], [], [], []
    for i in range(DEPTH):
        j = i // 2
        hn = rms_norm(h, W['mix_norm_pre'][i])
        if i % 2 == 0:
            cache_j = None if fox_cache is None else tuple(c[j] for c in fox_cache)
            y, k, v, lf, cst = conv_fox_mixer(hn, conv_prev[j], cache_j, W['e_w_in'][j], W['e_b_f'][j],
                                              W['e_conv_w'][j], W['e_w_out'][j])
            ks.append(k)
            vs.append(v)
            lfs.append(lf)
            convs.append(cst)
        else:
            y, sh, st = rwkv7_mixer(hn, shift_prev[j], wkv_prev[j], *[W[n][j] for n in RWKV_NAMES])
            shifts.append(sh)
            wkvs.append(st)
        h = h + rms_norm(y, W['mix_norm_post'][i])
        f = swiglu(rms_norm(h, W['ffn_norm_pre'][i]), W['f_w_gate'][i], W['f_w_up'][i], W['f_w_down'][i])
        h = h + rms_norm(f, W['ffn_norm_post'][i])
        gate = jax.nn.sigmoid(rms_norm(h, W['ple_norm'][i]) @ W['ple_gate'][i])
        h = h + gate * (p[i].astype(h.dtype) @ W['ple_proj'][i])
    return (h, jnp.stack(ks), jnp.stack(vs), jnp.stack(lfs), jnp.stack(convs), jnp.stack(shifts), jnp.stack(wkvs))


def setup_inputs(seed: int = 0) -> dict:
    key = jax.random.key(seed)
    ks = iter(jax.random.split(key, 64))
    nrm = lambda shape, s=1.0: jax.random.normal(next(ks), shape, jnp.float32) * s
    gain = lambda shape: 1.0 + nrm(shape, 0.05)
    D = D_MODEL
    return {
        'x_prompt': nrm((BATCH, SEQ, D)),
        'x_sample': nrm((DEC_BATCH, DEC_SEQ, D)),
        'p_prompt': nrm((DEPTH, BATCH, SEQ, PLE_DIM)),
        'p_sample': nrm((DEPTH, DEC_BATCH, DEC_SEQ, PLE_DIM)),
        'cache_k': nrm((N_EVEN, DEC_BATCH, PAST_LEN, H_F, HD_F)),
        'cache_v': nrm((N_EVEN, DEC_BATCH, PAST_LEN, H_F, HD_F)),
        'cache_logf': jax.nn.log_sigmoid(4.0 + nrm((N_EVEN, DEC_BATCH, PAST_LEN, H_F))),
        'state_conv': nrm((N_EVEN, DEC_BATCH, CONV_W - 1, D_A)),
        'state_shift': nrm((N_ODD, DEC_BATCH, D)),
        'state_wkv': nrm((N_ODD, DEC_BATCH, H_R, HD_R, HD_R), 0.5),
        'mix_norm_pre': gain((DEPTH, D)),
        'mix_norm_post': gain((DEPTH, D)),
        'ffn_norm_pre': gain((DEPTH, D)),
        'ffn_norm_post': gain((DEPTH, D)),
        'e_w_in': nrm((N_EVEN, D, IN_COLS), D ** -0.5),
        'e_b_f': 4.0 + nrm((N_EVEN, H_F), 0.1),
        'e_conv_w': nrm((N_EVEN, CONV_W, D_A), CONV_W ** -0.5),
        'e_w_out': nrm((N_EVEN, D_MIX, D), D_MIX ** -0.5),
        'r_mu': jax.random.uniform(next(ks), (N_ODD, 6, D), jnp.float32),
        'r_w_r': nrm((N_ODD, D, D), D ** -0.5),
        'r_w_k': nrm((N_ODD, D, D), D ** -0.5),
        'r_w_v': nrm((N_ODD, D, D), D ** -0.5),
        'r_w_o': nrm((N_ODD, D, D), D ** -0.5),
        'r_w0': 0.5 + nrm((N_ODD, D), 0.5),
        'r_w1': nrm((N_ODD, D, LORA_W), D ** -0.5),
        'r_w2': nrm((N_ODD, LORA_W, D), 0.1 * LORA_W ** -0.5),
        'r_a0': nrm((N_ODD, D), 0.1),
        'r_a1': nrm((N_ODD, D, LORA_A), D ** -0.5),
        'r_a2': nrm((N_ODD, LORA_A, D), 0.1 * LORA_A ** -0.5),
        'r_g1': nrm((N_ODD, D, LORA_G), D ** -0.5),
        'r_g2': nrm((N_ODD, LORA_G, D), LORA_G ** -0.5),
        'r_k_k': 0.85 + nrm((N_ODD, D), 0.05),
        'r_k_a': 1.0 + nrm((N_ODD, D), 0.05),
        'r_r_k': nrm((N_ODD, H_R, HD_R), 0.1),
        'r_ln_w': gain((N_ODD, D)),
        'r_ln_b': nrm((N_ODD, D), 0.02),
        'f_w_gate': nrm((DEPTH, D, D_FF), D ** -0.5),
        'f_w_up': nrm((DEPTH, D, D_FF), D ** -0.5),
        'f_w_down': nrm((DEPTH, D_FF, D), D_FF ** -0.5),
        'ple_norm': gain((DEPTH, D)),
        'ple_gate': nrm((DEPTH, D, D), D ** -0.5),
        'ple_proj': nrm((DEPTH, PLE_DIM, D), PLE_DIM ** -0.5),
    }


def reference(x_prompt, x_sample, p_prompt, p_sample, cache_k, cache_v, cache_logf, state_conv, state_shift,
              state_wkv, mix_norm_pre, mix_norm_post, ffn_norm_pre, ffn_norm_post, e_w_in, e_b_f, e_conv_w,
              e_w_out, r_mu, r_w_r, r_w_k, r_w_v, r_w_o, r_w0, r_w1, r_w2, r_a0, r_a1, r_a2, r_g1, r_g2,
              r_k_k, r_k_a, r_r_k, r_ln_w, r_ln_b, f_w_gate, f_w_up, f_w_down, ple_norm, ple_gate, ple_proj):
    W = dict(mix_norm_pre=mix_norm_pre, mix_norm_post=mix_norm_post, ffn_norm_pre=ffn_norm_pre,
             ffn_norm_post=ffn_norm_post, e_w_in=e_w_in, e_b_f=e_b_f, e_conv_w=e_conv_w, e_w_out=e_w_out,
             r_mu=r_mu, r_w_r=r_w_r, r_w_k=r_w_k, r_w_v=r_w_v, r_w_o=r_w_o, r_w0=r_w0, r_w1=r_w1, r_w2=r_w2,
             r_a0=r_a0, r_a1=r_a1, r_a2=r_a2, r_g1=r_g1, r_g2=r_g2, r_k_k=r_k_k, r_k_a=r_k_a, r_r_k=r_r_k,
             r_ln_w=r_ln_w, r_ln_b=r_ln_b, f_w_gate=f_w_gate, f_w_up=f_w_up, f_w_down=f_w_down,
             ple_norm=ple_norm, ple_gate=ple_gate, ple_proj=ple_proj)
    bp = x_prompt.shape[0]
    zero_conv = jnp.zeros((N_EVEN, bp, CONV_W - 1, D_A), x_prompt.dtype)
    zero_shift = jnp.zeros((N_ODD, bp, D_MODEL), x_prompt.dtype)
    zero_wkv = jnp.zeros((N_ODD, bp, H_R, HD_R, HD_R), jnp.float32)
    y_p, k_p, v_p, lf_p, c_p, sh_p, s_p = trunk(x_prompt, p_prompt, None, zero_conv, zero_shift, zero_wkv, W)
    y_s, k_s, v_s, lf_s, c_s, sh_s, s_s = trunk(x_sample, p_sample, (cache_k, cache_v, cache_logf),
                                                state_conv, state_shift, state_wkv, W)
    return (y_p, y_s, k_p, v_p, lf_p, c_p, sh_p, s_p, k_s, v_s, lf_s, c_s, sh_s, s_s)
```

```python
import functools

import jax
import jax.numpy as jnp
from jax import lax
from jax.experimental import pallas as pl
from jax.experimental.pallas import tpu as pltpu

D_MODEL = 1024
D_A = 512
H_F = 8
HD_F = 64
D_F = H_F * HD_F
HD_R = 64
H_R = D_MODEL // HD_R
PLE_DIM = 256
D_FF = 2816
NORM_EPS = 1e-6
GN_EPS = 64e-5
L2_EPS = 1e-12
NEG_INF = -1e30

LANES = 128
WKV_CHUNK = 64
VMEM_LIMIT = 48 * 1024 * 1024

BF16 = jnp.bfloat16
F32 = jnp.float32
HI = lax.Precision.HIGHEST


def _cparams(*sem):
    return pltpu.CompilerParams(dimension_semantics=sem, vmem_limit_bytes=VMEM_LIMIT)


def _tile(n, pref):
    t = min(n, pref)
    assert n % t == 0, (n, pref)
    return t


def _rms(x, g):
    return x * lax.rsqrt(jnp.mean(x * x, axis=-1, keepdims=True) + NORM_EPS) * g


def _bdot(a, w):
    return jnp.dot(a.astype(BF16), w, preferred_element_type=F32)


def _dot_nt(a, b, **kw):
    return lax.dot_general(a, b, (((1,), (1,)), ((), ())), preferred_element_type=F32, **kw)


def _dot_tn(a, b, **kw):
    return lax.dot_general(a, b, (((0,), (0,)), ((), ())), preferred_element_type=F32, **kw)


def _softplus(y):
    return jnp.maximum(y, 0.0) + jnp.log1p(jnp.exp(-jnp.abs(y)))


def _split3(x):
    hi = x.astype(BF16)
    r1 = x - hi.astype(F32)
    mid = r1.astype(BF16)
    lo = (r1 - mid.astype(F32)).astype(BF16)
    return hi, mid, lo


def _head_sum(x, sel):
    hi, mid, lo = _split3(x)
    return (jnp.dot(hi, sel, preferred_element_type=F32) + jnp.dot(mid, sel, preferred_element_type=F32)
            + jnp.dot(lo, sel, preferred_element_type=F32))


def _head_expand(x, selt):
    hi, mid, lo = _split3(x)
    return (jnp.dot(hi, selt, preferred_element_type=F32) + jnp.dot(mid, selt, preferred_element_type=F32)
            + jnp.dot(lo, selt, preferred_element_type=F32))


def _head_selectors(n_heads, hd):
    ch = jnp.arange(n_heads * hd) // hd
    sel = (ch[:, None] == jnp.arange(LANES)[None, :]).astype(BF16)
    return sel, sel.T


def _inproj_kernel(x_ref, g_ref, w_ref, wfl_ref, bf_ref, cprev_ref, cw_ref,
                   ya_ref, q_ref, k_ref, v_ref, kb_ref, vb_ref, lf_ref, cst_ref, carry_ref):
    s = pl.program_id(1)
    ts = x_ref.shape[0]
    xn = _rms(x_ref[...], g_ref[...]).astype(BF16)
    z = [jnp.dot(xn, w_ref[:, c * D_A:(c + 1) * D_A], preferred_element_type=F32) for c in range(6)]
    ax, a_b, a_c, q, k, v = z
    fl = jnp.dot(xn, wfl_ref[...], preferred_element_type=F32) + bf_ref[...]
    lf_ref[...] = -_softplus(-fl)

    @pl.when(s == 0)
    def _():
        carry_ref[6:8, :] = cprev_ref[...]

    u = a_c * ax
    p0 = carry_ref[6:7, :]
    p1 = carry_ref[7:8, :]
    row = lax.broadcasted_iota(jnp.int32, u.shape, 0)
    um1 = jnp.where(row == 0, p1, pltpu.roll(u, 1, 0))
    um2 = jnp.where(row == 0, p0, jnp.where(row == 1, p1, pltpu.roll(u, 2, 0)))
    cu = cw_ref[0:1, :] * um2 + cw_ref[1:2, :] * um1 + cw_ref[2:3, :] * u
    ya_ref[...] = (a_b * cu).astype(BF16)
    carry_ref[...] = u[ts - 8:ts, :]
    cst_ref[...] = u[ts - 2:ts, :]
    q_ref[...] = (q * (HD_F ** -0.5)).astype(BF16)
    k_ref[...] = k
    v_ref[...] = v
    kb_ref[...] = k.astype(BF16)
    vb_ref[...] = v.astype(BF16)


def _inproj(x, g, w_main, w_fl, b_f, conv_prev, conv_w):
    B, S, D = x.shape
    ts = _tile(S, 512)
    tok = lambda w: pl.BlockSpec((None, ts, w), lambda b, s: (b, s, 0))
    full = lambda a: pl.BlockSpec(a.shape, lambda b, s: (0,) * a.ndim)
    st = pl.BlockSpec((None, 2, D_A), lambda b, s: (b, 0, 0))
    out_shape = (
        jax.ShapeDtypeStruct((B, S, D_A), BF16),
        jax.ShapeDtypeStruct((B, S, D_F), BF16),
        jax.ShapeDtypeStruct((B, S, D_F), F32),
        jax.ShapeDtypeStruct((B, S, D_F), F32),
        jax.ShapeDtypeStruct((B, S, D_F), BF16),
        jax.ShapeDtypeStruct((B, S, D_F), BF16),
        jax.ShapeDtypeStruct((B, S, LANES), F32),
        jax.ShapeDtypeStruct((B, 2, D_A), F32),
    )
    return pl.pallas_call(
        _inproj_kernel,
        grid=(B, S // ts),
        in_specs=[tok(D), full(g), full(w_main), full(w_fl), full(b_f), st, full(conv_w)],
        out_specs=(tok(D_A), tok(D_F), tok(D_F), tok(D_F), tok(D_F), tok(D_F), tok(LANES), st),
        out_shape=out_shape,
        scratch_shapes=[pltpu.VMEM((8, D_A), F32)],
        compiler_params=_cparams("parallel", "arbitrary"),
        name="l0_inproj_conv",
    )(x, g, w_main, w_fl, b_f, conv_prev, conv_w)


def _cumsum_kernel(x_ref, o_ref):
    x = x_ref[...]
    n = x.shape[0]
    row = lax.broadcasted_iota(jnp.int32, x.shape, 0)
    sh = 1
    while sh < n:
        x = x + jnp.where(row >= sh, pltpu.roll(x, sh, 0), 0.0)
        sh *= 2
    o_ref[...] = x


def _cumsum(x):
    B, S, W = x.shape
    spec = pl.BlockSpec((None, S, W), lambda b: (b, 0, 0))
    return pl.pallas_call(
        _cumsum_kernel, grid=(B,), in_specs=[spec], out_specs=spec,
        out_shape=jax.ShapeDtypeStruct(x.shape, F32),
        compiler_params=_cparams("parallel"), name="logf_cumsum",
    )(x)


def _attn_block(q2, k2, v2, bias_fn, mask, m_ref, l_ref, acc_ref, pair):
    lane = lax.broadcasted_iota(jnp.int32, q2.shape, 1)
    for e in range(2):
        h = 2 * pair + e
        qm = jnp.where((lane < HD_F) == (e == 0), q2, jnp.zeros_like(q2))
        s = _dot_nt(qm, k2) + bias_fn(h)
        if mask is not None:
            s = jnp.where(mask, s, NEG_INF)
        m_old = m_ref[h]
        m_new = jnp.maximum(m_old, jnp.max(s, axis=-1, keepdims=True))
        alpha = jnp.exp(m_old - m_new)
        p = jnp.exp(s - m_new[:, 0:1])
        l_ref[h] = alpha * l_ref[h] + jnp.sum(p, axis=-1, keepdims=True)
        acc_ref[h] = alpha * acc_ref[h] + jnp.dot(p.astype(BF16), v2, preferred_element_type=F32)
        m_ref[h] = m_new


def _attn_init(m_ref, l_ref, acc_ref):
    m_ref[...] = jnp.full(m_ref.shape, NEG_INF, F32)
    l_ref[...] = jnp.zeros(l_ref.shape, F32)
    acc_ref[...] = jnp.zeros(acc_ref.shape, F32)


def _attn_finish(o_ref, l_ref, acc_ref):
    lane = lax.broadcasted_iota(jnp.int32, acc_ref.shape[1:], 1)
    for pair in range(H_F // 2):
        oa = acc_ref[2 * pair] / l_ref[2 * pair][:, 0:1]
        ob = acc_ref[2 * pair + 1] / l_ref[2 * pair + 1][:, 0:1]
        o_ref[:, pair * LANES:(pair + 1) * LANES] = jnp.where(lane < HD_F, oa, ob).astype(o_ref.dtype)


def _fox_prompt_kernel(q_ref, k_ref, v_ref, c_ref, ct_ref, o_ref, m_ref, l_ref, acc_ref):
    i = pl.program_id(1)
    j = pl.program_id(2)
    tq, tk = q_ref.shape[0], k_ref.shape[0]

    @pl.when(j == 0)
    def _():
        _attn_init(m_ref, l_ref, acc_ref)

    def update(mask):
        c = c_ref[...]
        ct = ct_ref[...]
        for pair in range(H_F // 2):
            sl = slice(pair * LANES, (pair + 1) * LANES)
            _attn_block(q_ref[:, sl], k_ref[:, sl], v_ref[:, sl],
                        lambda h: c[:, h:h + 1] - ct[h:h + 1, :], mask, m_ref, l_ref, acc_ref, pair)

    @pl.when(j < i)
    def _():
        update(None)

    @pl.when(j == i)
    def _():
        rq = lax.broadcasted_iota(jnp.int32, (tq, tk), 0)
        rk = lax.broadcasted_iota(jnp.int32, (tq, tk), 1)
        update(rq >= rk)
        _attn_finish(o_ref, l_ref, acc_ref)


def _fox_prompt(q, kb, vb, c, ct):
    B, S, _ = q.shape
    t = _tile(S, 512)
    n = S // t
    qs = pl.BlockSpec((None, t, D_F), lambda b, i, j: (b, i, 0))
    ks = pl.BlockSpec((None, t, D_F), lambda b, i, j: (b, jnp.minimum(j, i), 0))
    cs = pl.BlockSpec((None, t, LANES), lambda b, i, j: (b, i, 0))
    cts = pl.BlockSpec((None, 8, t), lambda b, i, j: (b, 0, jnp.minimum(j, i)))
    return pl.pallas_call(
        _fox_prompt_kernel, grid=(B, n, n),
        in_specs=[qs, ks, ks, cs, cts], out_specs=qs,
        out_shape=jax.ShapeDtypeStruct((B, S, D_F), BF16),
        scratch_shapes=[pltpu.VMEM((H_F, t, LANES), F32), pltpu.VMEM((H_F, t, LANES), F32),
                        pltpu.VMEM((H_F, t, LANES), F32)],
        compiler_params=_cparams("parallel", "parallel", "arbitrary"),
        name="fox_prompt_attention",
    )(q, kb, vb, c, ct)


def _fox_sample_kernel(q_ref, ck_ref, cv_ref, dt_ref, kn_ref, vn_ref, cn_ref, cnt_ref, o_ref,
                       m_ref, l_ref, acc_ref):
    j = pl.program_id(1)
    T = q_ref.shape[0]

    @pl.when(j == 0)
    def _():
        _attn_init(m_ref, l_ref, acc_ref)

    cn = cn_ref[...]
    dt = dt_ref[...]
    for pair in range(H_F // 2):
        sl = slice(pair * LANES, (pair + 1) * LANES)
        _attn_block(q_ref[:, sl], ck_ref[:, sl].astype(BF16), cv_ref[:, sl].astype(BF16),
                    lambda h: cn[:, h:h + 1] - dt[h:h + 1, :], None, m_ref, l_ref, acc_ref, pair)

    @pl.when(j == pl.num_programs(1) - 1)
    def _():
        cnt = cnt_ref[...]
        rq = lax.broadcasted_iota(jnp.int32, (T, T), 0)
        rk = lax.broadcasted_iota(jnp.int32, (T, T), 1)
        for pair in range(H_F // 2):
            sl = slice(pair * LANES, (pair + 1) * LANES)
            _attn_block(q_ref[:, sl], kn_ref[:, sl], vn_ref[:, sl],
                        lambda h: cn[:, h:h + 1] - cnt[h:h + 1, :], rq >= rk, m_ref, l_ref, acc_ref, pair)
        _attn_finish(o_ref, l_ref, acc_ref)


def _fox_sample(q, ck, cv, dt, kb, vb, cn, cnt):
    B, T, _ = q.shape
    P = ck.shape[1]
    tk = _tile(P, 1024)
    new = lambda w: pl.BlockSpec((None, T, w), lambda b, j: (b, 0, 0))
    cache = pl.BlockSpec((None, tk, D_F), lambda b, j: (b, j, 0))
    return pl.pallas_call(
        _fox_sample_kernel, grid=(B, P // tk),
        in_specs=[new(D_F), cache, cache, pl.BlockSpec((None, 8, tk), lambda b, j: (b, 0, j)),
                  new(D_F), new(D_F), new(LANES), pl.BlockSpec((None, 8, T), lambda b, j: (b, 0, 0))],
        out_specs=new(D_F),
        out_shape=jax.ShapeDtypeStruct((B, T, D_F), BF16),
        scratch_shapes=[pltpu.VMEM((H_F, T, LANES), F32), pltpu.VMEM((H_F, T, LANES), F32),
                        pltpu.VMEM((H_F, T, LANES), F32)],
        compiler_params=_cparams("parallel", "arbitrary"),
        name="fox_sample_attention",
    )(q, ck, cv, dt, kb, vb, cn, cnt)


def _row_call(kernel, tokens, consts, out_widths, out_dtypes, name, tm_pref=512):
    n = tokens[0].shape[0]
    tm = _tile(n, tm_pref)
    tspec = lambda w: pl.BlockSpec((tm, w), lambda i: (i, 0))
    cspec = lambda a: pl.BlockSpec(a.shape, lambda i: (0,) * a.ndim)
    outs = tuple(jax.ShapeDtypeStruct((n, w), d) for w, d in zip(out_widths, out_dtypes))
    res = pl.pallas_call(
        kernel, grid=(n // tm,),
        in_specs=[tspec(t.shape[1]) for t in tokens] + [cspec(c) for c in consts],
        out_specs=tuple(tspec(w) for w in out_widths),
        out_shape=outs,
        compiler_params=_cparams("parallel"), name=name,
    )(*tokens, *consts)
    return res


def _outproj_kernel(ya_ref, o_ref, h_ref, w_ref, g_ref, out_ref):
    y = (jnp.dot(ya_ref[...], w_ref[0:D_A, :], preferred_element_type=F32)
         + jnp.dot(o_ref[...], w_ref[D_A:D_A + D_F, :], preferred_element_type=F32))
    out_ref[...] = h_ref[...] + _rms(y, g_ref[...])


def _ffn_kernel(h_ref, gpre_ref, wg_ref, wu_ref, wd_ref, gpost_ref, out_ref, xn_ref, acc_ref):
    c = pl.program_id(1)

    @pl.when(c == 0)
    def _():
        xn_ref[...] = _rms(h_ref[...], gpre_ref[...]).astype(BF16)
        acc_ref[...] = jnp.zeros(acc_ref.shape, F32)

    xn = xn_ref[...]
    gate = jnp.dot(xn, wg_ref[...], preferred_element_type=F32)
    up = jnp.dot(xn, wu_ref[...], preferred_element_type=F32)
    act = (gate * jax.nn.sigmoid(gate) * up).astype(BF16)
    acc_ref[...] += jnp.dot(act, wd_ref[...], preferred_element_type=F32)

    @pl.when(c == pl.num_programs(1) - 1)
    def _():
        out_ref[...] = h_ref[...] + _rms(acc_ref[...], gpost_ref[...])


def _ffn(h, g_pre, w_gate, w_up, w_down, g_post):
    n, D = h.shape
    tm = _tile(n, 1024)
    tf = 256
    assert D_FF % tf == 0
    tok = pl.BlockSpec((tm, D), lambda i, c: (i, 0))
    vec = pl.BlockSpec((1, D), lambda i, c: (0, 0))
    return pl.pallas_call(
        _ffn_kernel, grid=(n // tm, D_FF // tf),
        in_specs=[tok, vec, pl.BlockSpec((D, tf), lambda i, c: (0, c)), pl.BlockSpec((D, tf), lambda i, c: (0, c)),
                  pl.BlockSpec((tf, D), lambda i, c: (c, 0)), vec],
        out_specs=tok,
        out_shape=jax.ShapeDtypeStruct((n, D), F32),
        scratch_shapes=[pltpu.VMEM((tm, D), BF16), pltpu.VMEM((tm, D), F32)],
        compiler_params=_cparams("parallel", "arbitrary"), name="swiglu_ffn",
    )(h, g_pre, w_gate, w_up, w_down, g_post)


def _ple_kernel(h_ref, p_ref, g_ref, wg_ref, wp_ref, out_ref):
    h = h_ref[...]
    gate = jax.nn.sigmoid(_bdot(_rms(h, g_ref[...]), wg_ref[...]))
    out_ref[...] = h + gate * _bdot(p_ref[...], wp_ref[...])


def _rwkv_prep_kernel(h_ref, sprev_ref, gpre_ref, mu_ref, wr_ref, wk_ref, wv_ref, w0_ref, w1_ref, w2_ref,
                      a0_ref, a1_ref, a2_ref, g1_ref, g2_ref, kk_ref, ka_ref, sel_ref, selt_ref,
                      r_out, lw_out, k_out, v_out, kk_out, b_out, g_out, shift_out, carry_ref):
    s = pl.program_id(1)
    ts = h_ref.shape[0]
    xn = _rms(h_ref[...], gpre_ref[...])

    @pl.when(s == 0)
    def _():
        carry_ref[7:8, :] = sprev_ref[...]

    row = lax.broadcasted_iota(jnp.int32, xn.shape, 0)
    xx = jnp.where(row == 0, carry_ref[7:8, :], pltpu.roll(xn, 1, 0)) - xn
    carry_ref[...] = xn[ts - 8:ts, :]
    shift_out[...] = xn[ts - 1:ts, :]
    mix = lambda n: xn + xx * mu_ref[n:n + 1, :]
    r = _bdot(mix(0), wr_ref[...])
    k = _bdot(mix(2), wk_ref[...])
    v = _bdot(mix(3), wv_ref[...])
    wl = w0_ref[...] + _bdot(jnp.tanh(_bdot(mix(1), w1_ref[...])), w2_ref[...])
    a = jax.nn.sigmoid(a0_ref[...] + _bdot(_bdot(mix(4), a1_ref[...]), a2_ref[...]))
    g = _bdot(jax.nn.sigmoid(_bdot(mix(5), g1_ref[...])), g2_ref[...])
    kkr = k * kk_ref[...]
    nrm = jnp.maximum(jnp.sqrt(_head_sum(kkr * kkr, sel_ref[...])), L2_EPS)
    kk = kkr * _head_expand(1.0 / nrm, selt_ref[...])
    r_out[...] = r
    lw_out[...] = -jnp.exp(-_softplus(-wl) - 0.5)
    k_out[...] = k * (1.0 + (a - 1.0) * ka_ref[...])
    v_out[...] = v
    kk_out[...] = kk
    b_out[...] = kk * a
    g_out[...] = g


def _rwkv_prep(h, shift_prev, consts):
    B, S, D = h.shape
    ts = _tile(S, 512)
    tok = pl.BlockSpec((None, ts, D), lambda b, s: (b, s, 0))
    row = pl.BlockSpec((None, 1, D), lambda b, s: (b, 0, 0))
    full = lambda a: pl.BlockSpec(a.shape, lambda b, s: (0,) * a.ndim)
    big = jax.ShapeDtypeStruct((B, S, D), F32)
    return pl.pallas_call(
        _rwkv_prep_kernel, grid=(B, S // ts),
        in_specs=[tok, row] + [full(c) for c in consts],
        out_specs=(tok,) * 7 + (row,),
        out_shape=(big,) * 7 + (jax.ShapeDtypeStruct((B, 1, D), F32),),
        scratch_shapes=[pltpu.VMEM((8, D), F32)],
        compiler_params=_cparams("parallel", "arbitrary"), name="rwkv_prep",
    )(h, shift_prev, *consts)


def _wkv_kernel(r_ref, lw_ref, k_ref, v_ref, kk_ref, b_ref, s0_ref, o_ref, s_ref):
    c = pl.program_id(1)
    L = r_ref.shape[0]

    @pl.when(c == 0)
    def _():
        s_ref[...] = s0_ref[...]

    ri = lax.broadcasted_iota(jnp.int32, (L, L), 0)
    ci = lax.broadcasted_iota(jnp.int32, (L, L), 1)
    incl = ri >= ci
    strict = ri > ci
    tril = incl.astype(F32)
    eye = (ri == ci).astype(F32)
    mm = functools.partial(jnp.dot, precision=HI, preferred_element_type=F32)
    for h in range(H_R):
        sl = slice(h * HD_R, (h + 1) * HD_R)
        r, lw, k, v, kk, b = (x[:, sl] for x in (r_ref, lw_ref, k_ref, v_ref, kk_ref, b_ref))
        S0 = s_ref[h]
        cum = mm(tril, lw)
        at = -kk * jnp.exp(cum - lw)
        w_inv = jnp.exp(-cum)
        bt = b * w_inv
        kt = k * w_inv
        rt = r * jnp.exp(cum)
        m_ab = jnp.where(strict, _dot_nt(at, bt, precision=HI), 0.0)
        m_ak = jnp.where(strict, _dot_nt(at, kt, precision=HI), 0.0)
        n_rb = jnp.where(incl, _dot_nt(rt, bt, precision=HI), 0.0)
        n_rk = jnp.where(incl, _dot_nt(rt, kt, precision=HI), 0.0)
        inv = eye + m_ab
        pw = m_ab
        sq = 2
        while sq < L:
            pw = mm(pw, pw)
            inv = inv + mm(pw, inv)
            sq *= 2
        u = mm(inv, _dot_nt(at, S0, precision=HI) + mm(m_ak, v))
        o_ref[:, sl] = _dot_nt(rt, S0, precision=HI) + mm(n_rb, u) + mm(n_rk, v)
        s_ref[h] = (S0 + _dot_tn(u, bt, precision=HI) + _dot_tn(v, kt, precision=HI)) * jnp.exp(cum[L - 1:L, :])


def _wkv(r, lw, k, v, kk, b, s0):
    B, S, D = r.shape
    L = _tile(S, WKV_CHUNK)
    tok = pl.BlockSpec((None, L, D), lambda b_, c: (b_, c, 0))
    st = pl.BlockSpec((None, H_R, HD_R, HD_R), lambda b_, c: (b_, 0, 0, 0))
    return pl.pallas_call(
        _wkv_kernel, grid=(B, S // L),
        in_specs=[tok] * 6 + [st], out_specs=(tok, st),
        out_shape=(jax.ShapeDtypeStruct((B, S, D), F32), jax.ShapeDtypeStruct(s0.shape, F32)),
        compiler_params=_cparams("parallel", "arbitrary"), name="wkv_chunked",
    )(r, lw, k, v, kk, b, s0)


def _rwkv_post_kernel(o_ref, r_ref, k_ref, v_ref, g_ref, h_ref, lnw_ref, lnb_ref, rk_ref, sel_ref, selt_ref,
                      wo_ref, gpost_ref, out_ref):
    sel, selt = sel_ref[...], selt_ref[...]
    o = o_ref[...]
    mean = _head_expand(_head_sum(o, sel) * (1.0 / HD_R), selt)
    d = o - mean
    var = _head_sum(d * d, sel) * (1.0 / HD_R)
    on = d * _head_expand(lax.rsqrt(var + GN_EPS), selt) * lnw_ref[...] + lnb_ref[...]
    bonus = _head_expand(_head_sum(r_ref[...] * k_ref[...] * rk_ref[...], sel), selt) * v_ref[...]
    y = _bdot((on + bonus).astype(BF16) * g_ref[...].astype(BF16), wo_ref[...])
    out_ref[...] = h_ref[...] + _rms(y, gpost_ref[...])


def _prep_weights(W):
    bf = lambda a: a.astype(BF16)
    row = lambda a: a.reshape(1, -1)
    P = dict(W)
    w_in = W["e_w_in"][0]
    n_main = 3 * D_A + 3 * D_F
    P["w_main"] = bf(w_in[:, :n_main])
    P["w_fl"] = bf(jnp.pad(w_in[:, n_main:], ((0, 0), (0, LANES - H_F))))
    P["b_f"] = jnp.pad(W["e_b_f"][0], (0, LANES - H_F)).reshape(1, LANES)
    P["w_out"] = bf(W["e_w_out"][0])
    for n in ("f_w_gate", "f_w_up", "f_w_down", "ple_gate", "ple_proj"):
        P[n] = bf(W[n])
    for n in ("r_w_r", "r_w_k", "r_w_v", "r_w_o", "r_w1", "r_w2", "r_a1", "r_a2", "r_g1", "r_g2"):
        P[n] = bf(W[n][0])
    for n in ("r_w0", "r_a0", "r_k_k", "r_k_a", "r_ln_w", "r_ln_b"):
        P[n] = row(W[n][0])
    P["r_r_k"] = W["r_r_k"][0].reshape(1, D_MODEL)
    P["r_mu"] = W["r_mu"][0]
    P["sel"], P["selt"] = _head_selectors(H_R, HD_R)
    return P


def _trunk(x, p, fox_cache, conv_prev, shift_prev, wkv_prev, P):
    B, S, D = x.shape
    n = B * S
    flat = lambda a: a.reshape(n, a.shape[-1])
    vec = lambda name, i: P[name][i].reshape(1, D)

    ya, q, k, v, kb, vb, lf, cst = _inproj(x, vec("mix_norm_pre", 0), P["w_main"], P["w_fl"], P["b_f"],
                                          conv_prev, P["e_conv_w"][0])
    c = _cumsum(lf)
    ct = jnp.swapaxes(c[:, :, :8], 1, 2)
    if fox_cache is None:
        o = _fox_prompt(q, kb, vb, c, ct)
    else:
        ck, cv, clf = fox_cache
        Pn = ck.shape[1]
        cc = _cumsum(jnp.pad(clf, ((0, 0), (0, 0), (0, LANES - H_F))))
        dt = jnp.swapaxes((cc - cc[:, Pn - 1:Pn, :])[:, :, :8], 1, 2)
        o = _fox_sample(q, ck.reshape(B, Pn, D_F), cv.reshape(B, Pn, D_F), dt, kb, vb, c, ct)
    (h,) = _row_call(_outproj_kernel, [flat(ya), flat(o), flat(x)], [P["w_out"], vec("mix_norm_post", 0)],
                     [D], [F32], "l0_outproj")
    h = _ffn(h, vec("ffn_norm_pre", 0), P["f_w_gate"][0], P["f_w_up"][0], P["f_w_down"][0], vec("ffn_norm_post", 0))
    (h,) = _row_call(_ple_kernel, [h, flat(p[0])], [vec("ple_norm", 0), P["ple_gate"][0], P["ple_proj"][0]],
                     [D], [F32], "ple0")

    consts = [vec("mix_norm_pre", 1), P["r_mu"], P["r_w_r"], P["r_w_k"], P["r_w_v"], P["r_w0"], P["r_w1"], P["r_w2"],
              P["r_a0"], P["r_a1"], P["r_a2"], P["r_g1"], P["r_g2"], P["r_k_k"], P["r_k_a"], P["sel"], P["selt"]]
    r, lw, km, vv, kk, bb, g, shift = _rwkv_prep(h.reshape(B, S, D), shift_prev.reshape(B, 1, D), consts)
    o1, wkv = _wkv(r, lw, km, vv, kk, bb, wkv_prev)
    (h,) = _row_call(_rwkv_post_kernel, [flat(o1), flat(r), flat(km), flat(vv), flat(g), h],
                     [P["r_ln_w"], P["r_ln_b"], P["r_r_k"], P["sel"], P["selt"], P["r_w_o"], vec("mix_norm_post", 1)],
                     [D], [F32], "rwkv_post")
    h = _ffn(h, vec("ffn_norm_pre", 1), P["f_w_gate"][1], P["f_w_up"][1], P["f_w_down"][1], vec("ffn_norm_post", 1))
    (h,) = _row_call(_ple_kernel, [h, flat(p[1])], [vec("ple_norm", 1), P["ple_gate"][1], P["ple_proj"][1]],
                     [D], [F32], "ple1")

    return (h.reshape(B, S, D), k.reshape(1, B, S, H_F, HD_F), v.reshape(1, B, S, H_F, HD_F),
            lf[None, :, :, :H_F], cst[None], shift.reshape(1, B, D), wkv[None])


def kernel(x_prompt, x_sample, p_prompt, p_sample, cache_k, cache_v, cache_logf, state_conv, state_shift, state_wkv, mix_norm_pre, mix_norm_post, ffn_norm_pre, ffn_norm_post, e_w_in, e_b_f, e_conv_w, e_w_out, r_mu, r_w_r, r_w_k, r_w_v, r_w_o, r_w0, r_w1, r_w2, r_a0, r_a1, r_a2, r_g1, r_g2, r_k_k, r_k_a, r_r_k, r_ln_w, r_ln_b, f_w_gate, f_w_up, f_w_down, ple_norm, ple_gate, ple_proj):
    W = dict(mix_norm_pre=mix_norm_pre, mix_norm_post=mix_norm_post, ffn_norm_pre=ffn_norm_pre,
             ffn_norm_post=ffn_norm_post, e_w_in=e_w_in, e_b_f=e_b_f, e_conv_w=e_conv_w, e_w_out=e_w_out,
             r_mu=r_mu, r_w_r=r_w_r, r_w_k=r_w_k, r_w_v=r_w_v, r_w_o=r_w_o, r_w0=r_w0, r_w1=r_w1, r_w2=r_w2,
             r_a0=r_a0, r_a1=r_a1, r_a2=r_a2, r_g1=r_g1, r_g2=r_g2, r_k_k=r_k_k, r_k_a=r_k_a, r_r_k=r_r_k,
             r_ln_w=r_ln_w, r_ln_b=r_ln_b, f_w_gate=f_w_gate, f_w_up=f_w_up, f_w_down=f_w_down,
             ple_norm=ple_norm, ple_gate=ple_gate, ple_proj=ple_proj)
    P = _prep_weights(W)
    bp = x_prompt.shape[0]
    y_p, k_p, v_p, lf_p, c_p, sh_p, s_p = _trunk(
        x_prompt, p_prompt, None, jnp.zeros((bp, 2, D_A), F32), jnp.zeros((bp, D_MODEL), F32),
        jnp.zeros((bp, H_R, HD_R, HD_R), F32), P)
    y_s, k_s, v_s, lf_s, c_s, sh_s, s_s = _trunk(
        x_sample, p_sample, (cache_k[0], cache_v[0], cache_logf[0]), state_conv[0], state_shift[0], state_wkv[0], P)
    return (y_p, y_s, k_p, v_p, lf_p, c_p, sh_p, s_p, k_s, v_s, lf_s, c_s, sh_s, s_s)
```

```python
import functools

import jax
import jax.numpy as jnp
import numpy as np
from jax import lax
from jax.experimental import pallas as pl
from jax.experimental.pallas import tpu as pltpu

D_MODEL = 1024
D_A = 512
H_F = 8
HD_F = 64
D_F = H_F * HD_F
HD_R = 64
H_R = D_MODEL // HD_R
PLE_DIM = 256
D_FF = 2816
NORM_EPS = 1e-6
GN_EPS = 64e-5
L2_EPS = 1e-12
NEG_INF = -1e30

LOG2E = 1.4426950408889634
LANES = 128
ATTN_Q_SUB = 256
WKV_CHUNK = 64
VMEM_LIMIT = 48 * 1024 * 1024

BF16 = jnp.bfloat16
F32 = jnp.float32
HI = lax.Precision.HIGHEST


def _cparams(*sem):
    return pltpu.CompilerParams(dimension_semantics=sem, vmem_limit_bytes=VMEM_LIMIT)


def _tile(n, pref):
    t = min(n, pref)
    assert n % t == 0, (n, pref)
    return t


def _rms(x, g):
    return x * lax.rsqrt(jnp.mean(x * x, axis=-1, keepdims=True) + NORM_EPS) * g


def _bdot(a, w):
    return jnp.dot(a.astype(BF16), w, preferred_element_type=F32)


def _dot_nt(a, b, **kw):
    return lax.dot_general(a, b, (((1,), (1,)), ((), ())), preferred_element_type=F32, **kw)


def _dot_tn(a, b, **kw):
    return lax.dot_general(a, b, (((0,), (0,)), ((), ())), preferred_element_type=F32, **kw)


def _softplus(y):
    return jnp.maximum(y, 0.0) + jnp.log1p(jnp.exp(-jnp.abs(y)))


def _split3(x):
    hi = x.astype(BF16)
    r1 = x - hi.astype(F32)
    mid = r1.astype(BF16)
    lo = (r1 - mid.astype(F32)).astype(BF16)
    return hi, mid, lo


def _head_sum(x, sel):
    hi, mid, lo = _split3(x)
    return (jnp.dot(hi, sel, preferred_element_type=F32) + jnp.dot(mid, sel, preferred_element_type=F32)
            + jnp.dot(lo, sel, preferred_element_type=F32))


def _head_expand(x, selt):
    hi, mid, lo = _split3(x)
    return (jnp.dot(hi, selt, preferred_element_type=F32) + jnp.dot(mid, selt, preferred_element_type=F32)
            + jnp.dot(lo, selt, preferred_element_type=F32))


def _head_selectors(n_heads, hd):
    ch = jnp.arange(n_heads * hd) // hd
    sel = (ch[:, None] == jnp.arange(LANES)[None, :]).astype(BF16)
    return sel, sel.T


def _inproj_kernel(x_ref, g_ref, w_ref, wfl_ref, bf_ref, cprev_ref, cw_ref,
                   ya_ref, q_ref, k_ref, v_ref, kb_ref, vb_ref, vt_ref, lf_ref, cst_ref, carry_ref):
    s = pl.program_id(1)
    ts = x_ref.shape[0]
    xn = _rms(x_ref[...], g_ref[...]).astype(BF16)
    z = [jnp.dot(xn, w_ref[:, c * D_A:(c + 1) * D_A], preferred_element_type=F32) for c in range(6)]
    ax, a_b, a_c, q, k, v = z
    fl = jnp.dot(xn, wfl_ref[...], preferred_element_type=F32) + bf_ref[...]
    lf_ref[...] = -_softplus(-fl)

    @pl.when(s == 0)
    def _():
        carry_ref[6:8, :] = cprev_ref[...]

    u = a_c * ax
    p0 = carry_ref[6:7, :]
    p1 = carry_ref[7:8, :]
    row = lax.broadcasted_iota(jnp.int32, u.shape, 0)
    um1 = jnp.where(row == 0, p1, pltpu.roll(u, 1, 0))
    um2 = jnp.where(row == 0, p0, jnp.where(row == 1, p1, pltpu.roll(u, 2, 0)))
    cu = cw_ref[0:1, :] * um2 + cw_ref[1:2, :] * um1 + cw_ref[2:3, :] * u
    ya_ref[...] = (a_b * cu).astype(BF16)
    carry_ref[...] = u[ts - 8:ts, :]
    cst_ref[...] = u[ts - 2:ts, :]
    q_ref[...] = (q * (HD_F ** -0.5 * LOG2E)).astype(BF16)
    k_ref[...] = k
    v_ref[...] = v
    kb_ref[...] = k.astype(BF16)
    vb_ref[...] = v.astype(BF16)
    vt_ref[...] = v.T.astype(BF16)


def _inproj(x, g, w_main, w_fl, b_f, conv_prev, conv_w):
    B, S, D = x.shape
    ts = _tile(S, 512)
    tok = lambda w: pl.BlockSpec((None, ts, w), lambda b, s: (b, s, 0))
    full = lambda a: pl.BlockSpec(a.shape, lambda b, s: (0,) * a.ndim)
    st = pl.BlockSpec((None, 2, D_A), lambda b, s: (b, 0, 0))
    out_shape = (
        jax.ShapeDtypeStruct((B, S, D_A), BF16),
        jax.ShapeDtypeStruct((B, S, D_F), BF16),
        jax.ShapeDtypeStruct((B, S, D_F), F32),
        jax.ShapeDtypeStruct((B, S, D_F), F32),
        jax.ShapeDtypeStruct((B, S, D_F), BF16),
        jax.ShapeDtypeStruct((B, S, D_F), BF16),
        jax.ShapeDtypeStruct((B, D_F, S), BF16),
        jax.ShapeDtypeStruct((B, S, LANES), F32),
        jax.ShapeDtypeStruct((B, 2, D_A), F32),
    )
    vt_spec = pl.BlockSpec((None, D_F, ts), lambda b, s: (b, 0, s))
    return pl.pallas_call(
        _inproj_kernel,
        grid=(B, S // ts),
        in_specs=[tok(D), full(g), full(w_main), full(w_fl), full(b_f), st, full(conv_w)],
        out_specs=(tok(D_A), tok(D_F), tok(D_F), tok(D_F), tok(D_F), tok(D_F), vt_spec, tok(LANES), st),
        out_shape=out_shape,
        scratch_shapes=[pltpu.VMEM((8, D_A), F32)],
        compiler_params=_cparams("parallel", "arbitrary"),
        name="l0_inproj_conv",
    )(x, g, w_main, w_fl, b_f, conv_prev, conv_w)


def _scan_rows(x):
    n = x.shape[0]
    row = lax.broadcasted_iota(jnp.int32, x.shape, 0)
    sh = 1
    while sh < n:
        x = x + jnp.where(row >= sh, pltpu.roll(x, sh, 0), 0.0)
        sh *= 2
    return x


def _cumsum_kernel(x_ref, o_ref):
    o_ref[...] = _scan_rows(x_ref[...]) * LOG2E


def _cumsum_carrier_kernel(x_ref, pq_ref, pk_ref, oq_ref, ok_ref, o_ref, qc_ref, kc_ref):
    c = _scan_rows(x_ref[...]) * LOG2E
    o_ref[...] = c
    parts = jnp.concatenate(_split3(c), axis=1)
    qc_ref[...] = (jnp.dot(parts, pq_ref[...], preferred_element_type=F32) + oq_ref[...]).astype(BF16)
    kc_ref[...] = (jnp.dot(parts, pk_ref[...], preferred_element_type=F32) + ok_ref[...]).astype(BF16)


def _carrier_lane(h, slot):
    return LANES * (h // 2) + (HD_F if h % 2 == 0 else 0) + slot


def _carrier_placement():
    pq =np.zeros((3 * LANES, D_F), np.float32)
    pk = np.zeros((3 * LANES, D_F), np.float32)
    oq = np.zeros((1, D_F), np.float32)
    ok = np.zeros((1, D_F), np.float32)
    for h in range(H_F):
        for part in range(3):
            pq[part * LANES + h, _carrier_lane(h, part)] = 1.0
            ok[0, _carrier_lane(h, part)] = 1.0
            pk[part * LANES + h, _carrier_lane(h, 3 + part)] = -1.0
            oq[0, _carrier_lane(h, 3 + part)] = 1.0
    return jnp.asarray(pq, BF16), jnp.asarray(pk, BF16), jnp.asarray(oq), jnp.asarray(ok)


def _cumsum(x, carriers):
    B, S, W = x.shape
    spec = pl.BlockSpec((None, S, W), lambda b: (b, 0, 0))
    if not carriers:
        return pl.pallas_call(
            _cumsum_kernel, grid=(B,), in_specs=[spec], out_specs=spec,
            out_shape=jax.ShapeDtypeStruct(x.shape, F32),
            compiler_params=_cparams("parallel"), name="logf_cumsum",
        )(x)
    consts = _carrier_placement()
    cspec = pl.BlockSpec((None, S, D_F), lambda b: (b, 0, 0))
    return pl.pallas_call(
        _cumsum_carrier_kernel, grid=(B,),
        in_specs=[spec] + [pl.BlockSpec(a.shape, lambda b: (0, 0)) for a in consts],
        out_specs=(spec, cspec, cspec),
        out_shape=(jax.ShapeDtypeStruct(x.shape, F32), jax.ShapeDtypeStruct((B, S, D_F), BF16),
                   jax.ShapeDtypeStruct((B, S, D_F), BF16)),
        compiler_params=_cparams("parallel"), name="logf_cumsum_carriers",
    )(x, *consts)


def _fox_prompt_kernel(q_ref, k_ref, vt_ref, qc_ref, kc_ref, o_ref, m_ref, acc_ref):
    i = pl.program_id(1)
    j = pl.program_id(2)
    tq, tk = q_ref.shape[0], k_ref.shape[0]
    tqs = min(tq, ATTN_Q_SUB)

    @pl.when(j == 0)
    def _():
        m_ref[...] = jnp.full(m_ref.shape, NEG_INF, F32)
        acc_ref[...] = jnp.zeros(acc_ref.shape, F32)

    def update(diag):
        lane_q = lax.broadcasted_iota(jnp.int32, (tq, LANES), 1)
        lane_k = lax.broadcasted_iota(jnp.int32, (tk, LANES), 1)
        row_v = lax.broadcasted_iota(jnp.int32, (LANES, tk), 0)
        for pair in range(H_F // 2):
            sl = slice(pair * LANES, (pair + 1) * LANES)
            q2, qc2, k2, kc2, vt2 = q_ref[:, sl], qc_ref[:, sl], k_ref[:, sl], kc_ref[:, sl], vt_ref[sl, :]
            own = lambda idx, e: (idx < HD_F) == (e == 0)
            qa = [jnp.where(own(lane_q, e), q2, qc2) for e in range(2)]
            ka = [jnp.where(own(lane_k, e), k2, kc2) for e in range(2)]
            va = [jnp.where(own(row_v, e), vt2, jnp.ones_like(vt2)) for e in range(2)]
            units = [(e, qs) for e in range(2) for qs in range(tq // tqs)]
            cs = [slice(qs * tqs, (qs + 1) * tqs) for _, qs in units]
            nk = [(qs + 1) * tqs if diag else tk for _, qs in units]
            st = [_dot_nt(ka[e][:nk[u]], qa[e][cs[u]]) for u, (e, _) in enumerate(units)]
            if diag:
                for u, (_, qs) in enumerate(units):
                    key = lax.broadcasted_iota(jnp.int32, (nk[u], tqs), 0)
                    qry = lax.broadcasted_iota(jnp.int32, (nk[u], tqs), 1) + qs * tqs
                    st[u] = jnp.where(key <= qry, st[u], NEG_INF)
            m_old = [m_ref[2 * pair + e, :, cs[u]] for u, (e, _) in enumerate(units)]
            m_new = [jnp.maximum(m_old[u], jnp.max(st[u], axis=0, keepdims=True)) for u in range(len(units))]
            pt = [jnp.exp2(st[u] - m_new[u]).astype(BF16) for u in range(len(units))]
            pv = [jnp.dot(va[e][:, :nk[u]], pt[u], preferred_element_type=F32) for u, (e, _) in enumerate(units)]
            for u, (e, _) in enumerate(units):
                h = 2 * pair + e
                acc_ref[h, :, cs[u]] = jnp.exp2(m_old[u] - m_new[u]) * acc_ref[h, :, cs[u]] + pv[u]
                m_ref[h, :, cs[u]] = m_new[u]

    @pl.when(j < i)
    def _():
        update(False)

    @pl.when(j == i)
    def _():
        update(True)
        for pair in range(H_F // 2):
            a, b = acc_ref[2 * pair], acc_ref[2 * pair + 1]
            ot = jnp.concatenate([a[:HD_F] / a[HD_F:HD_F + 1], b[HD_F:] / b[0:1]], axis=0)
            o_ref[:, pair * LANES:(pair + 1) * LANES] = ot.T.astype(o_ref.dtype)


def _fox_prompt(q, kb, vt, qc, kc):
    B, S, _ = q.shape
    t = _tile(S, 512)
    n = S // t
    qs = pl.BlockSpec((None, t, D_F), lambda b, i, j: (b, i, 0))
    ks = pl.BlockSpec((None, t, D_F), lambda b, i, j: (b, jnp.minimum(j, i), 0))
    vs = pl.BlockSpec((None, D_F, t), lambda b, i, j: (b, 0, jnp.minimum(j, i)))
    return pl.pallas_call(
        _fox_prompt_kernel, grid=(B, n, n),
        in_specs=[qs, ks, vs, qs, ks], out_specs=qs,
        out_shape=jax.ShapeDtypeStruct((B, S, D_F), BF16),
        scratch_shapes=[pltpu.VMEM((H_F, 1, t), F32), pltpu.VMEM((H_F, LANES, t), F32)],
        compiler_params=_cparams("parallel", "parallel", "arbitrary"),
        name="fox_prompt_attention",
    )(q, kb, vt, qc, kc)


def _fox_sample_kernel(q_ref, ck_hbm, cv_hbm, dt_ref, kn_ref, vn_ref, cn_ref, cnt_ref, o_ref,
                       kbuf, vbuf, sem, m_ref, l_ref, acc_ref):
    b, h, j = pl.program_id(0), pl.program_id(1), pl.program_id(2)
    nh, nk = pl.num_programs(1), pl.num_programs(2)
    T = q_ref.shape[0]
    tk = kbuf.shape[1]
    step_id = (b * nh + h) * nk + j
    slot = step_id % 2

    def copies(bb, hh, jj, sl):
        rows = pl.ds(jj * tk, tk)
        return (pltpu.make_async_copy(ck_hbm.at[bb, rows, hh], kbuf.at[sl], sem.at[0, sl]),
                pltpu.make_async_copy(cv_hbm.at[bb, rows, hh], vbuf.at[sl], sem.at[1, sl]))

    @pl.when(step_id == 0)
    def _():
        for cp in copies(b, h, j, slot):
            cp.start()

    @pl.when(step_id + 1 < pl.num_programs(0) * nh * nk)
    def _():
        nxt = step_id + 1
        for cp in copies(nxt // (nh * nk), (nxt // nk) % nh, nxt % nk, 1 - slot):
            cp.start()

    @pl.when(j == 0)
    def _():
        m_ref[...] = jnp.full(m_ref.shape, NEG_INF, F32)
        l_ref[...] = jnp.zeros(l_ref.shape, F32)
        acc_ref[...] = jnp.zeros(acc_ref.shape, F32)

    q = q_ref[...]
    cn = cn_ref[...]

    def step(kh, vh, bias_row, mask):
        s = _dot_nt(q, kh) + (cn - bias_row)
        if mask is not None:
            s = jnp.where(mask, s, NEG_INF)
        m_old = m_ref[...]
        m_new = jnp.maximum(m_old, jnp.max(s, axis=-1, keepdims=True))
        alpha = jnp.exp2(m_old - m_new)
        p = jnp.exp2(s - m_new)
        l_ref[...] = alpha * l_ref[...] + jnp.sum(p, axis=-1, keepdims=True)
        acc_ref[...] = alpha * acc_ref[...] + jnp.dot(p.astype(BF16), vh, preferred_element_type=F32)
        m_ref[...] = m_new

    for cp in copies(b, h, j, slot):
        cp.wait()
    step(kbuf[slot].astype(BF16), vbuf[slot].astype(BF16), dt_ref[...], None)

    @pl.when(j == nk - 1)
    def _():
        rq = lax.broadcasted_iota(jnp.int32, (T, T), 0)
        rk = lax.broadcasted_iota(jnp.int32, (T, T), 1)
        step(kn_ref[...], vn_ref[...], cnt_ref[...], rq >= rk)
        o_ref[...] = (acc_ref[...] / l_ref[...]).astype(o_ref.dtype)


def _fox_sample(q, ck, cv, dt, kb, vb, cn, cnt):
    B, T, _ = q.shape
    P = ck.shape[1]
    tk = _tile(P, 2048)
    heads = lambda a: jnp.swapaxes(a.reshape(B, T, H_F, HD_F), 1, 2)
    new = lambda w: pl.BlockSpec((None, None, T, w), lambda b, h, j: (b, h, 0, 0))
    hbm = pl.BlockSpec(memory_space=pl.ANY)
    o = pl.pallas_call(
        _fox_sample_kernel, grid=(B, H_F, P // tk),
        in_specs=[new(HD_F), hbm, hbm, pl.BlockSpec((None, None, 1, tk), lambda b, h, j: (b, h, 0, j)),
                  new(HD_F), new(HD_F), new(1), pl.BlockSpec((None, None, 1, T), lambda b, h, j: (b, h, 0, 0))],
        out_specs=new(HD_F),
        out_shape=jax.ShapeDtypeStruct((B, H_F, T, HD_F), BF16),
        scratch_shapes=[pltpu.VMEM((2, tk, HD_F), F32), pltpu.VMEM((2, tk, HD_F), F32),
                        pltpu.SemaphoreType.DMA((2, 2)),
                        pltpu.VMEM((T, 1), F32), pltpu.VMEM((T, 1), F32), pltpu.VMEM((T, HD_F), F32)],
        compiler_params=_cparams("arbitrary", "arbitrary", "arbitrary"),
        name="fox_sample_attention",
    )(heads(q), ck, cv, dt[:, :, None, :], heads(kb), heads(vb),
      jnp.swapaxes(cn[:, :, :H_F], 1, 2)[..., None], cnt[:, :, None, :])
    return jnp.swapaxes(o, 1, 2).reshape(B, T, D_F)


def _row_call(kernel, tokens, consts, out_widths, out_dtypes, name, tm_pref=512):
    n = tokens[0].shape[0]
    tm = _tile(n, tm_pref)
    tspec = lambda w: pl.BlockSpec((tm, w), lambda i: (i, 0))
    cspec = lambda a: pl.BlockSpec(a.shape, lambda i: (0,) * a.ndim)
    outs = tuple(jax.ShapeDtypeStruct((n, w), d) for w, d in zip(out_widths, out_dtypes))
    res = pl.pallas_call(
        kernel, grid=(n // tm,),
        in_specs=[tspec(t.shape[1]) for t in tokens] + [cspec(c) for c in consts],
        out_specs=tuple(tspec(w) for w in out_widths),
        out_shape=outs,
        compiler_params=_cparams("parallel"), name=name,
    )(*tokens, *consts)
    return res


def _outproj_kernel(ya_ref, o_ref, h_ref, w_ref, g_ref, out_ref):
    y = (jnp.dot(ya_ref[...], w_ref[0:D_A, :], preferred_element_type=F32)
         + jnp.dot(o_ref[...], w_ref[D_A:D_A + D_F, :], preferred_element_type=F32))
    out_ref[...] = h_ref[...] + _rms(y, g_ref[...])


def _ffn_kernel(h_ref, gpre_ref, wg_ref, wu_ref, wd_ref, gpost_ref, out_ref, xn_ref, acc_ref):
    c = pl.program_id(1)

    @pl.when(c == 0)
    def _():
        xn_ref[...] = _rms(h_ref[...], gpre_ref[...]).astype(BF16)
        acc_ref[...] = jnp.zeros(acc_ref.shape, F32)

    xn = xn_ref[...]
    gate = jnp.dot(xn, wg_ref[...], preferred_element_type=F32)
    up = jnp.dot(xn, wu_ref[...], preferred_element_type=F32)
    act = (gate * jax.nn.sigmoid(gate) * up).astype(BF16)
    acc_ref[...] += jnp.dot(act, wd_ref[...], preferred_element_type=F32)

    @pl.when(c == pl.num_programs(1) - 1)
    def _():
        out_ref[...] = h_ref[...] + _rms(acc_ref[...], gpost_ref[...])


def _ffn(h, g_pre, w_gate, w_up, w_down, g_post):
    n, D = h.shape
    tm = _tile(n, 1024)
    tf = 256
    assert D_FF % tf == 0
    tok = pl.BlockSpec((tm, D), lambda i, c: (i, 0))
    vec = pl.BlockSpec((1, D), lambda i, c: (0, 0))
    return pl.pallas_call(
        _ffn_kernel, grid=(n // tm, D_FF // tf),
        in_specs=[tok, vec, pl.BlockSpec((D, tf), lambda i, c: (0, c)), pl.BlockSpec((D, tf), lambda i, c: (0, c)),
                  pl.BlockSpec((tf, D), lambda i, c: (c, 0)), vec],
        out_specs=tok,
        out_shape=jax.ShapeDtypeStruct((n, D), F32),
        scratch_shapes=[pltpu.VMEM((tm, D), BF16), pltpu.VMEM((tm, D), F32)],
        compiler_params=_cparams("parallel", "arbitrary"), name="swiglu_ffn",
    )(h, g_pre, w_gate, w_up, w_down, g_post)


def _ple_kernel(h_ref, p_ref, g_ref, wg_ref, wp_ref, out_ref):
    h = h_ref[...]
    gate = jax.nn.sigmoid(_bdot(_rms(h, g_ref[...]), wg_ref[...]))
    out_ref[...] = h + gate * _bdot(p_ref[...], wp_ref[...])


def _rwkv_prep_kernel(h_ref, sprev_ref, gpre_ref, mu_ref, wr_ref, wk_ref, wv_ref, w0_ref, w1_ref, w2_ref,
                      a0_ref, a1_ref, a2_ref, g1_ref, g2_ref, kk_ref, ka_ref, sel_ref, selt_ref,
                      r_out, lw_out, k_out, v_out, kk_out, b_out, g_out, shift_out, carry_ref):
    s = pl.program_id(1)
    ts = h_ref.shape[0]
    xn = _rms(h_ref[...], gpre_ref[...])

    @pl.when(s == 0)
    def _():
        carry_ref[7:8, :] = sprev_ref[...]

    row = lax.broadcasted_iota(jnp.int32, xn.shape, 0)
    xx = jnp.where(row == 0, carry_ref[7:8, :], pltpu.roll(xn, 1, 0)) - xn
    carry_ref[...] = xn[ts - 8:ts, :]
    shift_out[...] = xn[ts - 1:ts, :]
    mix = lambda n: xn + xx * mu_ref[n:n + 1, :]
    r = _bdot(mix(0), wr_ref[...])
    k = _bdot(mix(2), wk_ref[...])
    v = _bdot(mix(3), wv_ref[...])
    wl = w0_ref[...] + _bdot(jnp.tanh(_bdot(mix(1), w1_ref[...])), w2_ref[...])
    a = jax.nn.sigmoid(a0_ref[...] + _bdot(_bdot(mix(4), a1_ref[...]), a2_ref[...]))
    g = _bdot(jax.nn.sigmoid(_bdot(mix(5), g1_ref[...])), g2_ref[...])
    kkr = k * kk_ref[...]
    nrm = jnp.maximum(jnp.sqrt(_head_sum(kkr * kkr, sel_ref[...])), L2_EPS)
    kk = kkr * _head_expand(1.0 / nrm, selt_ref[...])
    r_out[...] = r.astype(BF16)
    lw_out[...] = -jnp.exp(-_softplus(-wl) - 0.5)
    k_out[...] = (k * (1.0 + (a - 1.0) * ka_ref[...])).astype(BF16)
    v_out[...] = v.astype(BF16)
    kk_out[...] = kk.astype(BF16)
    b_out[...] = (kk * a).astype(BF16)
    g_out[...] = g.astype(BF16)


def _rwkv_prep(h, shift_prev, consts):
    B, S, D = h.shape
    ts = _tile(S, 512)
    tok = pl.BlockSpec((None, ts, D), lambda b, s: (b, s, 0))
    row = pl.BlockSpec((None, 1, D), lambda b, s: (b, 0, 0))
    full = lambda a: pl.BlockSpec(a.shape, lambda b, s: (0,) * a.ndim)
    big = lambda dt: jax.ShapeDtypeStruct((B, S, D), dt)
    return pl.pallas_call(
        _rwkv_prep_kernel, grid=(B, S // ts),
        in_specs=[tok, row] + [full(c) for c in consts],
        out_specs=(tok,) * 7 + (row,),
        out_shape=(big(BF16), big(F32)) + (big(BF16),) * 5 + (jax.ShapeDtypeStruct((B, 1, D), F32),),
        scratch_shapes=[pltpu.VMEM((8, D), F32)],
        compiler_params=_cparams("parallel", "arbitrary"), name="rwkv_prep",
    )(h, shift_prev, *consts)


def _wkv_kernel(r_ref, lw_ref, k_ref, v_ref, kk_ref, b_ref, s0_ref, o_ref, s_ref):
    c = pl.program_id(1)
    L = r_ref.shape[0]

    @pl.when(c == 0)
    def _():
        s_ref[...] = s0_ref[...]

    lw = lw_ref[...]
    row = lax.broadcasted_iota(jnp.int32, lw.shape, 0)
    cum = lw
    sh = 1
    while sh < L:
        cum = cum + jnp.where(row >= sh, pltpu.roll(cum, sh, 0), 0.0)
        sh *= 2
    w_inv = jnp.exp(-cum)
    at = (-kk_ref[...].astype(F32) * jnp.exp(cum - lw)).astype(BF16)
    rt = (r_ref[...].astype(F32) * jnp.exp(cum)).astype(BF16)
    bt = (b_ref[...].astype(F32) * w_inv).astype(BF16)
    kt = (k_ref[...].astype(F32) * w_inv).astype(BF16)
    ar = jnp.concatenate([at, rt], axis=0)
    bk = jnp.concatenate([bt, kt], axis=0)
    w_end = jnp.exp(cum[L - 1:L, :])

    ri = lax.broadcasted_iota(jnp.int32, (L, L), 0)
    ci = lax.broadcasted_iota(jnp.int32, (L, L), 1)
    incl = ri >= ci
    strict = ri > ci
    eye = (ri == ci).astype(F32)
    mm = functools.partial(jnp.dot, preferred_element_type=F32)
    heads = range(H_R)
    sls = [slice(h * HD_R, (h + 1) * HD_R) for h in heads]
    ar_h = [ar[:, sl] for sl in sls]
    bk_h = [bk[:, sl] for sl in sls]
    v_h = [v_ref[:, sl] for sl in sls]
    S0 = [s_ref[h] for h in heads]
    g_b = [_dot_nt(ar_h[h], bk_h[h][:L]) for h in heads]
    g_k = [_dot_nt(ar_h[h], bk_h[h][L:]) for h in heads]
    m_ab = [jnp.where(strict, g[:L], 0.0) for g in g_b]
    n_rb = [jnp.where(incl, g[L:], 0.0).astype(BF16) for g in g_b]
    mn_k = [jnp.concatenate([jnp.where(strict, g[:L], 0.0), jnp.where(incl, g[L:], 0.0)], axis=0).astype(BF16)
            for g in g_k]
    xs = [_dot_nt(ar_h[h], S0[h].astype(BF16)) + mm(mn_k[h], v_h[h]) for h in heads]
    inv = [eye + m for m in m_ab]
    pw = [m.astype(BF16) for m in m_ab]
    pw = [mm(p, p) for p in pw]
    sq = 2
    while sq < L:
        pwb = [p.astype(BF16) for p in pw]
        both = [mm(jnp.concatenate([inv[h].astype(BF16), pwb[h]], axis=0), pwb[h]) for h in heads]
        inv = [inv[h] + both[h][:L] for h in heads]
        pw = [b_[L:] for b_ in both]
        sq *= 2
    ub = [mm(inv[h].astype(BF16), xs[h][:L].astype(BF16)).astype(BF16) for h in heads]
    for h in heads:
        o_ref[:, sls[h]] = xs[h][L:] + mm(n_rb[h], ub[h])
    for h in heads:
        s_ref[h] = (S0[h] + _dot_tn(jnp.concatenate([ub[h], v_h[h]], axis=0), bk_h[h])) * w_end[:, sls[h]]


def _wkv(r, lw, k, v, kk, b, s0):
    B, S, D = r.shape
    L = _tile(S, WKV_CHUNK)
    tok = pl.BlockSpec((None, L, D), lambda b_, c: (b_, c, 0))
    st = pl.BlockSpec((None, H_R, HD_R, HD_R), lambda b_, c: (b_, 0, 0, 0))
    return pl.pallas_call(
        _wkv_kernel, grid=(B, S // L),
        in_specs=[tok] * 6 + [st], out_specs=(tok, st),
        out_shape=(jax.ShapeDtypeStruct((B, S, D), F32), jax.ShapeDtypeStruct(s0.shape, F32)),
        compiler_params=_cparams("parallel", "arbitrary"), name="wkv_chunked",
    )(r, lw, k, v, kk, b, s0)


def _rwkv_post_kernel(o_ref, r_ref, k_ref, v_ref, g_ref, h_ref, lnw_ref, lnb_ref, rk_ref, sel_ref, selt_ref,
                      wo_ref, gpost_ref, out_ref):
    sel, selt = sel_ref[...], selt_ref[...]
    o = o_ref[...]
    mean = _head_expand(_head_sum(o, sel) * (1.0 / HD_R), selt)
    d = o - mean
    var = _head_sum(d * d, sel) * (1.0 / HD_R)
    on = d * _head_expand(lax.rsqrt(var + GN_EPS), selt) * lnw_ref[...] + lnb_ref[...]
    rk = r_ref[...].astype(F32) * k_ref[...].astype(F32) * rk_ref[...]
    bonus = _head_expand(_head_sum(rk, sel), selt) * v_ref[...].astype(F32)
    y = _bdot((on + bonus) * g_ref[...].astype(F32), wo_ref[...])
    out_ref[...] = h_ref[...] + _rms(y, gpost_ref[...])


def _prep_weights(W):
    bf = lambda a: a.astype(BF16)
    row = lambda a: a.reshape(1, -1)
    P = dict(W)
    w_in = W["e_w_in"][0]
    n_main = 3 * D_A + 3 * D_F
    P["w_main"] = bf(w_in[:, :n_main])
    P["w_fl"] = bf(jnp.pad(w_in[:, n_main:], ((0, 0), (0, LANES - H_F))))
    P["b_f"] = jnp.pad(W["e_b_f"][0], (0, LANES - H_F)).reshape(1, LANES)
    P["w_out"] = bf(W["e_w_out"][0])
    for n in ("f_w_gate", "f_w_up", "f_w_down", "ple_gate", "ple_proj"):
        P[n] = bf(W[n])
    for n in ("r_w_r", "r_w_k", "r_w_v", "r_w_o", "r_w1", "r_w2", "r_a1", "r_a2", "r_g1", "r_g2"):
        P[n] = bf(W[n][0])
    for n in ("r_w0", "r_a0", "r_k_k", "r_k_a", "r_ln_w", "r_ln_b"):
        P[n] = row(W[n][0])
    P["r_r_k"] = W["r_r_k"][0].reshape(1, D_MODEL)
    P["r_mu"] = W["r_mu"][0]
    P["sel"], P["selt"] = _head_selectors(H_R, HD_R)
    return P


def _trunk(x, p, fox_cache, conv_prev, shift_prev, wkv_prev, P):
    B, S, D = x.shape
    n = B * S
    flat = lambda a: a.reshape(n, a.shape[-1])
    vec = lambda name, i: P[name][i].reshape(1, D)

    ya, q, k, v, kb, vb, vt, lf, cst = _inproj(x, vec("mix_norm_pre", 0), P["w_main"], P["w_fl"], P["b_f"],
                                              conv_prev, P["e_conv_w"][0])
    if fox_cache is None:
        _, qc, kc = _cumsum(lf, carriers=True)
        o = _fox_prompt(q, kb, vt, qc, kc)
    else:
        ck, cv, clf = fox_cache
        Pn = ck.shape[1]
        c = _cumsum(lf, carriers=False)
        ct = jnp.swapaxes(c[:, :, :8], 1, 2)
        cc = _cumsum(jnp.pad(clf, ((0, 0), (0, 0), (0, LANES - H_F))), carriers=False)
        dt = jnp.swapaxes((cc - cc[:, Pn - 1:Pn, :])[:, :, :8], 1, 2)
        o = _fox_sample(q, ck, cv, dt, kb, vb, c, ct)
    (h,) = _row_call(_outproj_kernel, [flat(ya), flat(o), flat(x)], [P["w_out"], vec("mix_norm_post", 0)],
                     [D], [F32], "l0_outproj")
    h = _ffn(h, vec("ffn_norm_pre", 0), P["f_w_gate"][0], P["f_w_up"][0], P["f_w_down"][0], vec("ffn_norm_post", 0))
    (h,) = _row_call(_ple_kernel, [h, flat(p[0])], [vec("ple_norm", 0), P["ple_gate"][0], P["ple_proj"][0]],
                     [D], [F32], "ple0")

    consts = [vec("mix_norm_pre", 1), P["r_mu"], P["r_w_r"], P["r_w_k"], P["r_w_v"], P["r_w0"], P["r_w1"], P["r_w2"],
              P["r_a0"], P["r_a1"], P["r_a2"], P["r_g1"], P["r_g2"], P["r_k_k"], P["r_k_a"], P["sel"], P["selt"]]
    r, lw, km, vv, kk, bb, g, shift = _rwkv_prep(h.reshape(B, S, D), shift_prev.reshape(B, 1, D), consts)
    o1, wkv = _wkv(r, lw, km, vv, kk, bb, wkv_prev)
    (h,) = _row_call(_rwkv_post_kernel, [flat(o1), flat(r), flat(km), flat(vv), flat(g), h],
                     [P["r_ln_w"], P["r_ln_b"], P["r_r_k"], P["sel"], P["selt"], P["r_w_o"], vec("mix_norm_post", 1)],
                     [D], [F32], "rwkv_post")
    h = _ffn(h, vec("ffn_norm_pre", 1), P["f_w_gate"][1], P["f_w_up"][1], P["f_w_down"][1], vec("ffn_norm_post", 1))
    (h,) = _row_call(_ple_kernel, [h, flat(p[1])], [vec("ple_norm", 1), P["ple_gate"][1], P["ple_proj"][1]],
                     [D], [F32], "ple1")

    return (h.reshape(B, S, D), k.reshape(1, B, S, H_F, HD_F), v.reshape(1, B, S, H_F, HD_F),
            lf[None, :, :, :H_F], cst[None], shift.reshape(1, B, D), wkv[None])


def kernel(x_prompt, x_sample, p_prompt, p_sample, cache_k, cache_v, cache_logf, state_conv, state_shift, state_wkv, mix_norm_pre, mix_norm_post, ffn_norm_pre, ffn_norm_post, e_w_in, e_b_f, e_conv_w, e_w_out, r_mu, r_w_r, r_w_k, r_w_v, r_w_o, r_w0, r_w1, r_w2, r_a0, r_a1, r_a2, r_g1, r_g2, r_k_k, r_k_a, r_r_k, r_ln_w, r_ln_b, f_w_gate, f_w_up, f_w_down, ple_norm, ple_gate, ple_proj):
    W = dict(mix_norm_pre=mix_norm_pre, mix_norm_post=mix_norm_post, ffn_norm_pre=ffn_norm_pre,
             ffn_norm_post=ffn_norm_post, e_w_in=e_w_in, e_b_f=e_b_f, e_conv_w=e_conv_w, e_w_out=e_w_out,
             r_mu=r_mu, r_w_r=r_w_r, r_w_k=r_w_k, r_w_v=r_w_v, r_w_o=r_w_o, r_w0=r_w0, r_w1=r_w1, r_w2=r_w2,
             r_a0=r_a0, r_a1=r_a1, r_a2=r_a2, r_g1=r_g1, r_g2=r_g2, r_k_k=r_k_k, r_k_a=r_k_a, r_r_k=r_r_k,
             r_ln_w=r_ln_w, r_ln_b=r_ln_b, f_w_gate=f_w_gate, f_w_up=f_w_up, f_w_down=f_w_down,
             ple_norm=ple_norm, ple_gate=ple_gate, ple_proj=ple_proj)
    P = _prep_weights(W)
    bp = x_prompt.shape[0]
    y_p, k_p, v_p, lf_p, c_p, sh_p, s_p = _trunk(
        x_prompt, p_prompt, None, jnp.zeros((bp, 2, D_A), F32), jnp.zeros((bp, D_MODEL), F32),
        jnp.zeros((bp, H_R, HD_R, HD_R), F32), P)
    y_s, k_s, v_s, lf_s, c_s, sh_s, s_s = _trunk(
        x_sample, p_sample, (cache_k[0], cache_v[0], cache_logf[0]), state_conv[0], state_shift[0], state_wkv[0], P)
    return (y_p, y_s, k_p, v_p, lf_p, c_p, sh_p, s_p, k_s, v_s, lf_s, c_s, sh_s, s_s)
```

```python
import functools

import jax
import jax.numpy as jnp
import numpy as np
from jax import lax
from jax.experimental import pallas as pl
from jax.experimental.pallas import tpu as pltpu

D_MODEL = 1024
D_A = 512
H_F = 8
HD_F = 64
D_F = H_F * HD_F
HD_R = 64
H_R = D_MODEL // HD_R
PLE_DIM = 256
D_FF = 2816
NORM_EPS = 1e-6
GN_EPS = 64e-5
L2_EPS = 1e-12
NEG_INF = -1e30

LOG2E = 1.4426950408889634
LANES = 128
FFN_CHUNK = 256
ATTN_Q_SUB = 256
WKV_CHUNK = 64
VMEM_LIMIT = 48 * 1024 * 1024

BF16 = jnp.bfloat16
F32 = jnp.float32


def _cparams(*sem):
    return pltpu.CompilerParams(dimension_semantics=sem, vmem_limit_bytes=VMEM_LIMIT)


def _tile(n, pref):
    t = min(n, pref)
    assert n % t == 0, (n, pref)
    return t


def _rms(x, g):
    return x * lax.rsqrt(jnp.mean(x * x, axis=-1, keepdims=True) + NORM_EPS) * g


def _bdot(a, w):
    return jnp.dot(a.astype(BF16), w, preferred_element_type=F32)


def _dot_nt(a, b, **kw):
    return lax.dot_general(a, b, (((1,), (1,)), ((), ())), preferred_element_type=F32, **kw)


def _dot_tn(a, b, **kw):
    return lax.dot_general(a, b, (((0,), (0,)), ((), ())), preferred_element_type=F32, **kw)


def _softplus(y):
    return jnp.maximum(y, 0.0) + jnp.log1p(jnp.exp(-jnp.abs(y)))


def _split3(x):
    hi = x.astype(BF16)
    r1 = x - hi.astype(F32)
    mid = r1.astype(BF16)
    lo = (r1 - mid.astype(F32)).astype(BF16)
    return hi, mid, lo


def _head_allsum(x, bd):
    hi = x.astype(BF16)
    lo = (x - hi.astype(F32)).astype(BF16)
    w = bd.shape[0]
    slabs = [jnp.dot(hi[:, t:t + w], bd, preferred_element_type=F32)
             + jnp.dot(lo[:, t:t + w], bd, preferred_element_type=F32) for t in range(0, x.shape[1], w)]
    return jnp.concatenate(slabs, axis=1)


def _head_blockdiag(hd, width=256):
    idx = jnp.arange(width) // hd
    return (idx[:, None] == idx[None, :]).astype(BF16)


def _inproj_kernel(x_ref, g_ref, w_ref, wfl_ref, bf_ref, cprev_ref, cw_ref,
                   ya_ref, q_ref, k_ref, v_ref, kb_ref, vb_ref, vt_ref, lf_ref, cst_ref, carry_ref):
    s = pl.program_id(1)
    ts = x_ref.shape[0]
    xn = _rms(x_ref[...], g_ref[...]).astype(BF16)
    z = [jnp.dot(xn, w_ref[:, c * D_A:(c + 1) * D_A], preferred_element_type=F32) for c in range(6)]
    ax, a_b, a_c, q, k, v = z
    fl = jnp.dot(xn, wfl_ref[...], preferred_element_type=F32) + bf_ref[...]
    lf_ref[...] = -_softplus(-fl)

    @pl.when(s == 0)
    def _():
        carry_ref[6:8, :] = cprev_ref[...]

    u = a_c * ax
    p0 = carry_ref[6:7, :]
    p1 = carry_ref[7:8, :]
    row = lax.broadcasted_iota(jnp.int32, u.shape, 0)
    um1 = jnp.where(row == 0, p1, pltpu.roll(u, 1, 0))
    um2 = jnp.where(row == 0, p0, jnp.where(row == 1, p1, pltpu.roll(u, 2, 0)))
    cu = cw_ref[0:1, :] * um2 + cw_ref[1:2, :] * um1 + cw_ref[2:3, :] * u
    ya_ref[...] = (a_b * cu).astype(BF16)
    carry_ref[...] = u[ts - 8:ts, :]
    cst_ref[...] = u[ts - 2:ts, :]
    q_ref[...] = (q * (HD_F ** -0.5 * LOG2E)).astype(BF16)
    k_ref[...] = k
    v_ref[...] = v
    kb_ref[...] = k.astype(BF16)
    vb_ref[...] = v.astype(BF16)
    vt_ref[...] = v.T.astype(BF16)


def _inproj(x, g, w_main, w_fl, b_f, conv_prev, conv_w):
    B, S, D = x.shape
    ts = _tile(S, 512)
    tok = lambda w: pl.BlockSpec((None, ts, w), lambda b, s: (b, s, 0))
    full = lambda a: pl.BlockSpec(a.shape, lambda b, s: (0,) * a.ndim)
    st = pl.BlockSpec((None, 2, D_A), lambda b, s: (b, 0, 0))
    out_shape = (
        jax.ShapeDtypeStruct((B, S, D_A), BF16),
        jax.ShapeDtypeStruct((B, S, D_F), BF16),
        jax.ShapeDtypeStruct((B, S, D_F), F32),
        jax.ShapeDtypeStruct((B, S, D_F), F32),
        jax.ShapeDtypeStruct((B, S, D_F), BF16),
        jax.ShapeDtypeStruct((B, S, D_F), BF16),
        jax.ShapeDtypeStruct((B, D_F, S), BF16),
        jax.ShapeDtypeStruct((B, S, LANES), F32),
        jax.ShapeDtypeStruct((B, 2, D_A), F32),
    )
    vt_spec = pl.BlockSpec((None, D_F, ts), lambda b, s: (b, 0, s))
    return pl.pallas_call(
        _inproj_kernel,
        grid=(B, S // ts),
        in_specs=[tok(D), full(g), full(w_main), full(w_fl), full(b_f), st, full(conv_w)],
        out_specs=(tok(D_A), tok(D_F), tok(D_F), tok(D_F), tok(D_F), tok(D_F), vt_spec, tok(LANES), st),
        out_shape=out_shape,
        scratch_shapes=[pltpu.VMEM((8, D_A), F32)],
        compiler_params=_cparams("parallel", "arbitrary"),
        name="l0_inproj_conv",
    )(x, g, w_main, w_fl, b_f, conv_prev, conv_w)


def _scan_rows(x):
    n = x.shape[0]
    row = lax.broadcasted_iota(jnp.int32, x.shape, 0)
    sh = 1
    while sh < n:
        x = x + jnp.where(row >= sh, pltpu.roll(x, sh, 0), 0.0)
        sh *= 2
    return x


def _cumsum_kernel(x_ref, o_ref):
    o_ref[...] = _scan_rows(x_ref[...]) * LOG2E


def _cumsum_carrier_kernel(x_ref, pq_ref, pk_ref, oq_ref, ok_ref, o_ref, qc_ref, kc_ref):
    c = _scan_rows(x_ref[...]) * LOG2E
    o_ref[...] = c
    parts = jnp.concatenate(_split3(c), axis=1)
    qc_ref[...] = (jnp.dot(parts, pq_ref[...], preferred_element_type=F32) + oq_ref[...]).astype(BF16)
    kc_ref[...] = (jnp.dot(parts, pk_ref[...], preferred_element_type=F32) + ok_ref[...]).astype(BF16)


def _carrier_lane(h, slot):
    return LANES * (h // 2) + (HD_F if h % 2 == 0 else 0) + slot


def _carrier_placement():
    pq = np.zeros((3 * LANES, D_F), np.float32)
    pk = np.zeros((3 * LANES, D_F), np.float32)
    oq = np.zeros((1, D_F), np.float32)
    ok = np.zeros((1, D_F), np.float32)
    for h in range(H_F):
        for part in range(3):
            pq[part * LANES + h, _carrier_lane(h, part)] = 1.0
            ok[0, _carrier_lane(h, part)] = 1.0
            pk[part * LANES + h, _carrier_lane(h, 3 + part)] = -1.0
            oq[0, _carrier_lane(h, 3 + part)] = 1.0
    return jnp.asarray(pq, BF16), jnp.asarray(pk, BF16), jnp.asarray(oq), jnp.asarray(ok)


def _cumsum(x, carriers):
    B, S, W = x.shape
    spec = pl.BlockSpec((None, S, W), lambda b: (b, 0, 0))
    if not carriers:
        return pl.pallas_call(
            _cumsum_kernel, grid=(B,), in_specs=[spec], out_specs=spec,
            out_shape=jax.ShapeDtypeStruct(x.shape, F32),
            compiler_params=_cparams("parallel"), name="logf_cumsum",
        )(x)
    consts = _carrier_placement()
    cspec = pl.BlockSpec((None, S, D_F), lambda b: (b, 0, 0))
    return pl.pallas_call(
        _cumsum_carrier_kernel, grid=(B,),
        in_specs=[spec] + [pl.BlockSpec(a.shape, lambda b: (0, 0)) for a in consts],
        out_specs=(spec, cspec, cspec),
        out_shape=(jax.ShapeDtypeStruct(x.shape, F32), jax.ShapeDtypeStruct((B, S, D_F), BF16),
                   jax.ShapeDtypeStruct((B, S, D_F), BF16)),
        compiler_params=_cparams("parallel"), name="logf_cumsum_carriers",
    )(x, *consts)


def _fox_prompt_kernel(q_ref, k_ref, vt_ref, qc_ref, kc_ref, o_ref, m_ref, acc_ref):
    i = pl.program_id(1)
    j = pl.program_id(2)
    tq, tk = q_ref.shape[0], k_ref.shape[0]
    tqs = min(tq, ATTN_Q_SUB)

    @pl.when(j == 0)
    def _():
        m_ref[...] = jnp.full(m_ref.shape, NEG_INF, F32)
        acc_ref[...] = jnp.zeros(acc_ref.shape, F32)

    def update(diag):
        lane_q = lax.broadcasted_iota(jnp.int32, (tq, LANES), 1)
        lane_k = lax.broadcasted_iota(jnp.int32, (tk, LANES), 1)
        row_v = lax.broadcasted_iota(jnp.int32, (LANES, tk), 0)
        for pair in range(H_F // 2):
            sl = slice(pair * LANES, (pair + 1) * LANES)
            q2, qc2, k2, kc2, vt2 = q_ref[:, sl], qc_ref[:, sl], k_ref[:, sl], kc_ref[:, sl], vt_ref[sl, :]
            own = lambda idx, e: (idx < HD_F) == (e == 0)
            qa = [jnp.where(own(lane_q, e), q2, qc2) for e in range(2)]
            ka = [jnp.where(own(lane_k, e), k2, kc2) for e in range(2)]
            va = [jnp.where(own(row_v, e), vt2, jnp.ones_like(vt2)) for e in range(2)]
            units = [(e, qs) for e in range(2) for qs in range(tq // tqs)]
            cs = [slice(qs * tqs, (qs + 1) * tqs) for _, qs in units]
            nk = [(qs + 1) * tqs if diag else tk for _, qs in units]
            st = [_dot_nt(ka[e][:nk[u]], qa[e][cs[u]]) for u, (e, _) in enumerate(units)]
            if diag:
                for u, (_, qs) in enumerate(units):
                    key = lax.broadcasted_iota(jnp.int32, (nk[u], tqs), 0)
                    qry = lax.broadcasted_iota(jnp.int32, (nk[u], tqs), 1) + qs * tqs
                    st[u] = jnp.where(key <= qry, st[u], NEG_INF)
            m_old = [m_ref[2 * pair + e, :, cs[u]] for u, (e, _) in enumerate(units)]
            m_new = [jnp.maximum(m_old[u], jnp.max(st[u], axis=0, keepdims=True)) for u in range(len(units))]
            pt = [jnp.exp2(st[u] - m_new[u]).astype(BF16) for u in range(len(units))]
            pv = [jnp.dot(va[e][:, :nk[u]], pt[u], preferred_element_type=F32) for u, (e, _) in enumerate(units)]
            for u, (e, _) in enumerate(units):
                h = 2 * pair + e
                acc_ref[h, :, cs[u]] = jnp.exp2(m_old[u] - m_new[u]) * acc_ref[h, :, cs[u]] + pv[u]
                m_ref[h, :, cs[u]] = m_new[u]

    @pl.when(j < i)
    def _():
        update(False)

    @pl.when(j == i)
    def _():
        update(True)
        for pair in range(H_F // 2):
            a, b = acc_ref[2 * pair], acc_ref[2 * pair + 1]
            ot = jnp.concatenate([a[:HD_F] / a[HD_F:HD_F + 1], b[HD_F:] / b[0:1]], axis=0)
            o_ref[:, pair * LANES:(pair + 1) * LANES] = ot.T.astype(o_ref.dtype)


def _fox_prompt(q, kb, vt, qc, kc):
    B, S, _ = q.shape
    t = _tile(S, 512)
    n = S // t
    qs = pl.BlockSpec((None, t, D_F), lambda b, i, j: (b, i, 0))
    ks = pl.BlockSpec((None, t, D_F), lambda b, i, j: (b, jnp.minimum(j, i), 0))
    vs = pl.BlockSpec((None, D_F, t), lambda b, i, j: (b, 0, jnp.minimum(j, i)))
    return pl.pallas_call(
        _fox_prompt_kernel, grid=(B, n, n),
        in_specs=[qs, ks, vs, qs, ks], out_specs=qs,
        out_shape=jax.ShapeDtypeStruct((B, S, D_F), BF16),
        scratch_shapes=[pltpu.VMEM((H_F, 1, t), F32), pltpu.VMEM((H_F, LANES, t), F32)],
        compiler_params=_cparams("parallel", "parallel", "arbitrary"),
        name="fox_prompt_attention",
    )(q, kb, vt, qc, kc)


def _fox_sample_kernel(q_ref, ckt_ref, cvt_ref, dt_ref, kn_ref, vn_ref, cn_ref, cnt_ref, o_ref,
                       m_ref, l_ref, acc_ref):
    j = pl.program_id(2)
    T = q_ref.shape[0]

    @pl.when(j == 0)
    def _():
        m_ref[...] = jnp.full(m_ref.shape, NEG_INF, F32)
        l_ref[...] = jnp.zeros(l_ref.shape, F32)
        acc_ref[...] = jnp.zeros(acc_ref.shape, F32)

    q = q_ref[...]
    cn = cn_ref[...]

    def step(s, pv, mask):
        if mask is not None:
            s = jnp.where(mask, s, NEG_INF)
        m_old = m_ref[...]
        m_new = jnp.maximum(m_old, jnp.max(s, axis=-1, keepdims=True))
        alpha = jnp.exp2(m_old - m_new)
        p = jnp.exp2(s - m_new)
        l_ref[...] = alpha * l_ref[...] + jnp.sum(p, axis=-1, keepdims=True)
        acc_ref[...] = alpha * acc_ref[...] + pv(p.astype(BF16))
        m_ref[...] = m_new

    kt = ckt_ref[...].astype(BF16)
    vt = cvt_ref[...].astype(BF16)
    step(jnp.dot(q, kt, preferred_element_type=F32) + (cn - dt_ref[...]), lambda p: _dot_nt(p, vt), None)

    @pl.when(j == pl.num_programs(2) - 1)
    def _():
        rq = lax.broadcasted_iota(jnp.int32, (T, T), 0)
        rk = lax.broadcasted_iota(jnp.int32, (T, T), 1)
        vn = vn_ref[...]
        step(_dot_nt(q, kn_ref[...]) + (cn - cnt_ref[...]),
             lambda p: jnp.dot(p, vn, preferred_element_type=F32), rq >= rk)
        o_ref[...] = (acc_ref[...] / l_ref[...]).astype(o_ref.dtype)


def _fox_sample(q, ckt, cvt, dt, kb, vb, cn, cnt):
    B, T, _ = q.shape
    P = ckt.shape[-1]
    tk = _tile(P, 4096)
    heads = lambda a: jnp.swapaxes(a.reshape(B, T, H_F, HD_F), 1, 2)
    new = lambda w: pl.BlockSpec((None, None, T, w), lambda b, h, j: (b, h, 0, 0))
    cache = pl.BlockSpec((None, None, HD_F, tk), lambda b, h, j: (b, h, 0, j))
    o = pl.pallas_call(
        _fox_sample_kernel, grid=(B, H_F, P // tk),
        in_specs=[new(HD_F), cache, cache, pl.BlockSpec((None, None, 1, tk), lambda b, h, j: (b, h, 0, j)),
                  new(HD_F), new(HD_F), new(1), pl.BlockSpec((None, None, 1, T), lambda b, h, j: (b, h, 0, 0))],
        out_specs=new(HD_F),
        out_shape=jax.ShapeDtypeStruct((B, H_F, T, HD_F), BF16),
        scratch_shapes=[pltpu.VMEM((T, 1), F32), pltpu.VMEM((T, 1), F32), pltpu.VMEM((T, HD_F), F32)],
        compiler_params=_cparams("parallel", "parallel", "arbitrary"),
        name="fox_sample_attention",
    )(heads(q), ckt, cvt, dt[:, :, None, :], heads(kb), heads(vb),
      jnp.swapaxes(cn[:, :, :H_F], 1, 2)[..., None], cnt[:, :, None, :])
    return jnp.swapaxes(o, 1, 2).reshape(B, T, D_F)


def _row_call(kernel, tokens, consts, out_widths, out_dtypes, name, tm_pref=512):
    n = tokens[0].shape[0]
    tm = _tile(n, tm_pref)
    tspec = lambda w: pl.BlockSpec((tm, w), lambda i: (i, 0))
    cspec = lambda a: pl.BlockSpec(a.shape, lambda i: (0,) * a.ndim)
    outs = tuple(jax.ShapeDtypeStruct((n, w), d) for w, d in zip(out_widths, out_dtypes))
    res = pl.pallas_call(
        kernel, grid=(n // tm,),
        in_specs=[tspec(t.shape[1]) for t in tokens] + [cspec(c) for c in consts],
        out_specs=tuple(tspec(w) for w in out_widths),
        out_shape=outs,
        compiler_params=_cparams("parallel"), name=name,
    )(*tokens, *consts)
    return res


def _outproj_kernel(ya_ref, o_ref, h_ref, w_ref, g_ref, out_ref):
    y = (jnp.dot(ya_ref[...], w_ref[0:D_A, :], preferred_element_type=F32)
         + jnp.dot(o_ref[...], w_ref[D_A:D_A + D_F, :], preferred_element_type=F32))
    out_ref[...] = h_ref[...] + _rms(y, g_ref[...])


def _ffn_ple_kernel(h_ref, p_ref, gpre_ref, wg_ref, wu_ref, wd_ref, gpost_ref, gple_ref, wpg_ref, wpp_ref,
                    out_ref, act_ref):
    h = h_ref[...]
    xn = _rms(h, gpre_ref[...]).astype(BF16)
    for c in range(0, D_FF, FFN_CHUNK):
        gate = jnp.dot(xn, wg_ref[:, c:c + FFN_CHUNK], preferred_element_type=F32)
        up = jnp.dot(xn, wu_ref[:, c:c + FFN_CHUNK], preferred_element_type=F32)
        act_ref[:, c:c + FFN_CHUNK] = (gate * jax.nn.sigmoid(gate) * up).astype(BF16)
    h = h + _rms(jnp.dot(act_ref[...], wd_ref[...], preferred_element_type=F32), gpost_ref[...])
    gate = jax.nn.sigmoid(_bdot(_rms(h, gple_ref[...]), wpg_ref[...]))
    out_ref[...] = h + gate * _bdot(p_ref[...], wpp_ref[...])


def _ffn_ple(h, p, g_pre, w_gate, w_up, w_down, g_post, g_ple, w_pgate, w_pproj):
    n, D = h.shape
    tm = _tile(n, 512)
    tok = lambda w: pl.BlockSpec((tm, w), lambda i: (i, 0))
    res = lambda a: pl.BlockSpec(a.shape, lambda i: (0,) * a.ndim, pipeline_mode=pl.Buffered(1))
    consts = [g_pre, w_gate, w_up, w_down, g_post, g_ple, w_pgate, w_pproj]
    return pl.pallas_call(
        _ffn_ple_kernel, grid=(n // tm,),
        in_specs=[tok(D), tok(p.shape[1])] + [res(c) for c in consts],
        out_specs=tok(D),
        out_shape=jax.ShapeDtypeStruct((n, D), F32),
        scratch_shapes=[pltpu.VMEM((tm, D_FF), BF16)],
        compiler_params=_cparams("parallel"), name="swiglu_ffn_ple",
    )(h, p, *consts)


def _rwkv_prep_kernel(h_ref, sprev_ref, gpre_ref, mu_ref, wr_ref, wk_ref, wv_ref, w0_ref, w1_ref, w2_ref,
                      a0_ref, a1_ref, a2_ref, g1_ref, g2_ref, kk_ref, ka_ref, bd_ref,
                      r_out, lw_out, k_out, v_out, kk_out, b_out, g_out, shift_out, carry_ref):
    s = pl.program_id(1)
    ts = h_ref.shape[0]
    xn = _rms(h_ref[...], gpre_ref[...])

    @pl.when(s == 0)
    def _():
        carry_ref[7:8, :] = sprev_ref[...]

    row = lax.broadcasted_iota(jnp.int32, xn.shape, 0)
    xx = jnp.where(row == 0, carry_ref[7:8, :], pltpu.roll(xn, 1, 0)) - xn
    carry_ref[...] = xn[ts - 8:ts, :]
    shift_out[...] = xn[ts - 1:ts, :]
    mix = lambda n: xn + xx * mu_ref[n:n + 1, :]
    r = _bdot(mix(0), wr_ref[...])
    k = _bdot(mix(2), wk_ref[...])
    v = _bdot(mix(3), wv_ref[...])
    wl = w0_ref[...] + _bdot(jnp.tanh(_bdot(mix(1), w1_ref[...])), w2_ref[...])
    a = jax.nn.sigmoid(a0_ref[...] + _bdot(_bdot(mix(4), a1_ref[...]), a2_ref[...]))
    g = _bdot(jax.nn.sigmoid(_bdot(mix(5), g1_ref[...])), g2_ref[...])
    kkr = k * kk_ref[...]
    kk = kkr / jnp.maximum(jnp.sqrt(_head_allsum(kkr * kkr, bd_ref[...])), L2_EPS)
    r_out[...] = r.astype(BF16)
    lw_out[...] = -jnp.exp(-_softplus(-wl) - 0.5)
    k_out[...] = (k * (1.0 + (a - 1.0) * ka_ref[...])).astype(BF16)
    v_out[...] = v.astype(BF16)
    kk_out[...] = kk.astype(BF16)
    b_out[...] = (kk * a).astype(BF16)
    g_out[...] = g.astype(BF16)


def _rwkv_prep(h, shift_prev, consts):
    B, S, D = h.shape
    ts = _tile(S, 512)
    tok = pl.BlockSpec((None, ts, D), lambda b, s: (b, s, 0))
    row = pl.BlockSpec((None, 1, D), lambda b, s: (b, 0, 0))
    full = lambda a: pl.BlockSpec(a.shape, lambda b, s: (0,) * a.ndim)
    big = lambda dt: jax.ShapeDtypeStruct((B, S, D), dt)
    return pl.pallas_call(
        _rwkv_prep_kernel, grid=(B, S // ts),
        in_specs=[tok, row] + [full(c) for c in consts],
        out_specs=(tok,) * 7 + (row,),
        out_shape=(big(BF16), big(F32)) + (big(BF16),) * 5 + (jax.ShapeDtypeStruct((B, 1, D), F32),),
        scratch_shapes=[pltpu.VMEM((8, D), F32)],
        compiler_params=_cparams("parallel", "arbitrary"), name="rwkv_prep",
    )(h, shift_prev, *consts)


def _wkv_kernel(r_ref, lw_ref, k_ref, v_ref, kk_ref, b_ref, s0_ref, o_ref, s_ref):
    c = pl.program_id(1)
    L = r_ref.shape[0]

    @pl.when(c == 0)
    def _():
        s_ref[...] = s0_ref[...]

    lw = lw_ref[...]
    row = lax.broadcasted_iota(jnp.int32, lw.shape, 0)
    cum = lw
    sh = 1
    while sh < L:
        cum = cum + jnp.where(row >= sh, pltpu.roll(cum, sh, 0), 0.0)
        sh *= 2
    w_inv = jnp.exp(-cum)
    at = (-kk_ref[...].astype(F32) * jnp.exp(cum - lw)).astype(BF16)
    rt = (r_ref[...].astype(F32) * jnp.exp(cum)).astype(BF16)
    bt = (b_ref[...].astype(F32) * w_inv).astype(BF16)
    kt = (k_ref[...].astype(F32) * w_inv).astype(BF16)
    ar = jnp.concatenate([at, rt], axis=0)
    bk = jnp.concatenate([bt, kt], axis=0)
    w_end = jnp.exp(cum[L - 1:L, :])

    ri = lax.broadcasted_iota(jnp.int32, (L, L), 0)
    ci = lax.broadcasted_iota(jnp.int32, (L, L), 1)
    incl = ri >= ci
    strict = ri > ci
    eye = (ri == ci).astype(F32)
    mm = functools.partial(jnp.dot, preferred_element_type=F32)
    heads = range(H_R)
    sls = [slice(h * HD_R, (h + 1) * HD_R) for h in heads]
    ar_h = [ar[:, sl] for sl in sls]
    bk_h = [bk[:, sl] for sl in sls]
    v_h = [v_ref[:, sl] for sl in sls]
    S0 = [s_ref[h] for h in heads]
    g_b = [_dot_nt(ar_h[h], bk_h[h][:L]) for h in heads]
    g_k = [_dot_nt(ar_h[h], bk_h[h][L:]) for h in heads]
    m_ab = [jnp.where(strict, g[:L], 0.0) for g in g_b]
    n_rb = [jnp.where(incl, g[L:], 0.0).astype(BF16) for g in g_b]
    mn_k = [jnp.concatenate([jnp.where(strict, g[:L], 0.0), jnp.where(incl, g[L:], 0.0)], axis=0).astype(BF16)
            for g in g_k]
    xs = [_dot_nt(ar_h[h], S0[h].astype(BF16)) + mm(mn_k[h], v_h[h]) for h in heads]
    inv = [eye + m for m in m_ab]
    pw = [m.astype(BF16) for m in m_ab]
    pw = [mm(p, p) for p in pw]
    sq = 2
    while sq < L:
        pwb = [p.astype(BF16) for p in pw]
        both = [mm(jnp.concatenate([inv[h].astype(BF16), pwb[h]], axis=0), pwb[h]) for h in heads]
        inv = [inv[h] + both[h][:L] for h in heads]
        pw = [b_[L:] for b_ in both]
        sq *= 2
    ub = [mm(inv[h].astype(BF16), xs[h][:L].astype(BF16)).astype(BF16) for h in heads]
    for h in heads:
        o_ref[:, sls[h]] = xs[h][L:] + mm(n_rb[h], ub[h])
    for h in heads:
        s_ref[h] = (S0[h] + _dot_tn(jnp.concatenate([ub[h], v_h[h]], axis=0), bk_h[h])) * w_end[:, sls[h]]


def _wkv(r, lw, k, v, kk, b, s0):
    B, S, D = r.shape
    L = _tile(S, WKV_CHUNK)
    tok = pl.BlockSpec((None, L, D), lambda b_, c: (b_, c, 0))
    st = pl.BlockSpec((None, H_R, HD_R, HD_R), lambda b_, c: (b_, 0, 0, 0))
    return pl.pallas_call(
        _wkv_kernel, grid=(B, S // L),
        in_specs=[tok] * 6 + [st], out_specs=(tok, st),
        out_shape=(jax.ShapeDtypeStruct((B, S, D), F32), jax.ShapeDtypeStruct(s0.shape, F32)),
        compiler_params=_cparams("parallel", "arbitrary"), name="wkv_chunked",
    )(r, lw, k, v, kk, b, s0)


def _rwkv_post_kernel(o_ref, r_ref, k_ref, v_ref, g_ref, h_ref, lnw_ref, lnb_ref, rk_ref, bd_ref,
                      wo_ref, gpost_ref, out_ref):
    bd = bd_ref[...]
    o = o_ref[...]
    d = o - _head_allsum(o, bd) * (1.0 / HD_R)
    var = _head_allsum(d * d, bd) * (1.0 / HD_R)
    on = d * lax.rsqrt(var + GN_EPS) * lnw_ref[...] + lnb_ref[...]
    rk = r_ref[...].astype(F32) * k_ref[...].astype(F32) * rk_ref[...]
    bonus = _head_allsum(rk, bd) * v_ref[...].astype(F32)
    y = _bdot((on + bonus) * g_ref[...].astype(F32), wo_ref[...])
    out_ref[...] = h_ref[...] + _rms(y, gpost_ref[...])


def _prep_weights(W):
    bf = lambda a: a.astype(BF16)
    row = lambda a: a.reshape(1, -1)
    P = dict(W)
    w_in = W["e_w_in"][0]
    n_main = 3 * D_A + 3 * D_F
    P["w_main"] = bf(w_in[:, :n_main])
    P["w_fl"] = bf(jnp.pad(w_in[:, n_main:], ((0, 0), (0, LANES - H_F))))
    P["b_f"] = jnp.pad(W["e_b_f"][0], (0, LANES - H_F)).reshape(1, LANES)
    P["w_out"] = bf(W["e_w_out"][0])
    for n in ("f_w_gate", "f_w_up", "f_w_down", "ple_gate", "ple_proj"):
        P[n] = bf(W[n])
    for n in ("r_w_r", "r_w_k", "r_w_v", "r_w_o", "r_w1", "r_w2", "r_a1", "r_a2", "r_g1", "r_g2"):
        P[n] = bf(W[n][0])
    for n in ("r_w0", "r_a0", "r_k_k", "r_k_a", "r_ln_w", "r_ln_b"):
        P[n] = row(W[n][0])
    P["r_r_k"] = W["r_r_k"][0].reshape(1, D_MODEL)
    P["r_mu"] = W["r_mu"][0]
    P["bd"] = _head_blockdiag(HD_R)
    return P


def _trunk(x, p, fox_cache, conv_prev, shift_prev, wkv_prev, P):
    B, S, D = x.shape
    n = B * S
    flat = lambda a: a.reshape(n, a.shape[-1])
    vec = lambda name, i: P[name][i].reshape(1, D)

    ya, q, k, v, kb, vb, vt, lf, cst = _inproj(x, vec("mix_norm_pre", 0), P["w_main"], P["w_fl"], P["b_f"],
                                              conv_prev, P["e_conv_w"][0])
    if fox_cache is None:
        _, qc, kc = _cumsum(lf, carriers=True)
        o = _fox_prompt(q, kb, vt, qc, kc)
    else:
        ck, cv, clf = fox_cache
        Pn = ck.shape[1]
        c = _cumsum(lf, carriers=False)
        ct = jnp.swapaxes(c[:, :, :8], 1, 2)
        cc = _cumsum(jnp.pad(clf, ((0, 0), (0, 0), (0, LANES - H_F))), carriers=False)
        dt = jnp.swapaxes((cc - cc[:, Pn - 1:Pn, :])[:, :, :8], 1, 2)
        to_t = lambda a: jnp.transpose(a, (0, 2, 3, 1))
        o = _fox_sample(q, to_t(ck), to_t(cv), dt, kb, vb, c, ct)
    (h,) = _row_call(_outproj_kernel, [flat(ya), flat(o), flat(x)], [P["w_out"], vec("mix_norm_post", 0)],
                     [D], [F32], "l0_outproj")
    tail = lambda h_, i: _ffn_ple(h_, flat(p[i]), vec("ffn_norm_pre", i), P["f_w_gate"][i], P["f_w_up"][i],
                                  P["f_w_down"][i], vec("ffn_norm_post", i), vec("ple_norm", i), P["ple_gate"][i],
                                  P["ple_proj"][i])
    h = tail(h, 0)

    consts = [vec("mix_norm_pre", 1), P["r_mu"], P["r_w_r"], P["r_w_k"], P["r_w_v"], P["r_w0"], P["r_w1"], P["r_w2"],
              P["r_a0"], P["r_a1"], P["r_a2"], P["r_g1"], P["r_g2"], P["r_k_k"], P["r_k_a"], P["bd"]]
    r, lw, km, vv, kk, bb, g, shift = _rwkv_prep(h.reshape(B, S, D), shift_prev.reshape(B, 1, D), consts)
    o1, wkv = _wkv(r, lw, km, vv, kk, bb, wkv_prev)
    (h,) = _row_call(_rwkv_post_kernel, [flat(o1), flat(r), flat(km), flat(vv), flat(g), h],
                     [P["r_ln_w"], P["r_ln_b"], P["r_r_k"], P["bd"], P["r_w_o"], vec("mix_norm_post", 1)],
                     [D], [F32], "rwkv_post")
    h = tail(h, 1)

    return (h.reshape(B, S, D), k.reshape(1, B, S, H_F, HD_F), v.reshape(1, B, S, H_F, HD_F),
            lf[None, :, :, :H_F], cst[None], shift.reshape(1, B, D), wkv[None])


def kernel(x_prompt, x_sample, p_prompt, p_sample, cache_k, cache_v, cache_logf, state_conv, state_shift, state_wkv, mix_norm_pre, mix_norm_post, ffn_norm_pre, ffn_norm_post, e_w_in, e_b_f, e_conv_w, e_w_out, r_mu, r_w_r, r_w_k, r_w_v, r_w_o, r_w0, r_w1, r_w2, r_a0, r_a1, r_a2, r_g1, r_g2, r_k_k, r_k_a, r_r_k, r_ln_w, r_ln_b, f_w_gate, f_w_up, f_w_down, ple_norm, ple_gate, ple_proj):
    W = dict(mix_norm_pre=mix_norm_pre, mix_norm_post=mix_norm_post, ffn_norm_pre=ffn_norm_pre,
             ffn_norm_post=ffn_norm_post, e_w_in=e_w_in, e_b_f=e_b_f, e_conv_w=e_conv_w, e_w_out=e_w_out,
             r_mu=r_mu, r_w_r=r_w_r, r_w_k=r_w_k, r_w_v=r_w_v, r_w_o=r_w_o, r_w0=r_w0, r_w1=r_w1, r_w2=r_w2,
             r_a0=r_a0, r_a1=r_a1, r_a2=r_a2, r_g1=r_g1, r_g2=r_g2, r_k_k=r_k_k, r_k_a=r_k_a, r_r_k=r_r_k,
             r_ln_w=r_ln_w, r_ln_b=r_ln_b, f_w_gate=f_w_gate, f_w_up=f_w_up, f_w_down=f_w_down,
             ple_norm=ple_norm, ple_gate=ple_gate, ple_proj=ple_proj)
    P = _prep_weights(W)
    bp = x_prompt.shape[0]
    y_p, k_p, v_p, lf_p, c_p, sh_p, s_p = _trunk(
        x_prompt, p_prompt, None, jnp.zeros((bp, 2, D_A), F32), jnp.zeros((bp, D_MODEL), F32),
        jnp.zeros((bp, H_R, HD_R, HD_R), F32), P)
    y_s, k_s, v_s, lf_s, c_s, sh_s, s_s = _trunk(
        x_sample, p_sample, (cache_k[0], cache_v[0], cache_logf[0]), state_conv[0], state_shift[0], state_wkv[0], P)
    return (y_p, y_s, k_p, v_p, lf_p, c_p, sh_p, s_p, k_s, v_s, lf_s, c_s, sh_s, s_s)
```

```python
import functools

import jax
import jax.numpy as jnp
import numpy as np
from jax import lax
from jax.experimental import pallas as pl
from jax.experimental.pallas import tpu as pltpu

D_MODEL = 1024
D_A = 512
H_F = 8
HD_F = 64
D_F = H_F * HD_F
HD_R = 64
H_R = D_MODEL // HD_R
PLE_DIM = 256
D_FF = 2816
NORM_EPS = 1e-6
GN_EPS = 64e-5
L2_EPS = 1e-12
NEG_INF = -1e30

LOG2E = 1.4426950408889634
DECAY_SCALE = 0.6065306597126334
LANES = 128
FFN_CHUNK = 256
ATTN_TILE = 1024
ATTN_Q_SUB = 256
WKV_SEQS = 2
WKV_CHUNK = 64
VMEM_LIMIT = 48 * 1024 * 1024

BF16 = jnp.bfloat16
F32 = jnp.float32


def _cparams(*sem):
    return pltpu.CompilerParams(dimension_semantics=sem, vmem_limit_bytes=VMEM_LIMIT)


def _tile(n, pref):
    t = min(n, pref)
    assert n % t == 0, (n, pref)
    return t


def _rms(x, g):
    return x * lax.rsqrt(jnp.mean(x * x, axis=-1, keepdims=True) + NORM_EPS) * g


def _bdot(a, w):
    return jnp.dot(a.astype(BF16), w, preferred_element_type=F32)


def _dot_nt(a, b, **kw):
    return lax.dot_general(a, b, (((1,), (1,)), ((), ())), preferred_element_type=F32, **kw)


def _dot_tn(a, b, **kw):
    return lax.dot_general(a, b, (((0,), (0,)), ((), ())), preferred_element_type=F32, **kw)


def _softplus(y):
    return jnp.maximum(y, 0.0) + jnp.log1p(jnp.exp(-jnp.abs(y)))


def _split3(x):
    hi = x.astype(BF16)
    r1 = x - hi.astype(F32)
    mid = r1.astype(BF16)
    lo = (r1 - mid.astype(F32)).astype(BF16)
    return hi, mid, lo


def _head_allsum(x, bd):
    hi = x.astype(BF16)
    lo = (x - hi.astype(F32)).astype(BF16)
    w = bd.shape[0]
    slabs = [jnp.dot(hi[:, t:t + w], bd, preferred_element_type=F32)
             + jnp.dot(lo[:, t:t + w], bd, preferred_element_type=F32) for t in range(0, x.shape[1], w)]
    return jnp.concatenate(slabs, axis=1)


def _head_blockdiag(hd, width=256):
    idx = jnp.arange(width) // hd
    return (idx[:, None] == idx[None, :]).astype(BF16)


def _inproj_kernel(x_ref, g_ref, w_ref, wfl_ref, bf_ref, cprev_ref, cw_ref,
                   ya_ref, q_ref, k_ref, v_ref, kb_ref, vb_ref, vt_ref, lf_ref, cst_ref, carry_ref):
    ts = x_ref.shape[0]

    @pl.when(pl.program_id(1) == 0)
    def _():
        carry_ref[6:8, :] = cprev_ref[...]

    xn = _rms(x_ref[...], g_ref[...]).astype(BF16)
    z = [jnp.dot(xn, w_ref[:, c * D_A:(c + 1) * D_A], preferred_element_type=F32) for c in range(6)]
    ax, a_b, a_c, q, k, v = z
    fl = jnp.dot(xn, wfl_ref[...], preferred_element_type=F32) + bf_ref[...]
    lf_ref[...] = -_softplus(-fl)
    u = a_c * ax
    p0 = carry_ref[6:7, :]
    p1 = carry_ref[7:8, :]
    row = lax.broadcasted_iota(jnp.int32, u.shape, 0)
    um1 = jnp.where(row == 0, p1, pltpu.roll(u, 1, 0))
    um2 = jnp.where(row == 0, p0, jnp.where(row == 1, p1, pltpu.roll(u, 2, 0)))
    cu = cw_ref[0:1, :] * um2 + cw_ref[1:2, :] * um1 + cw_ref[2:3, :] * u
    ya_ref[...] = (a_b * cu).astype(BF16)
    carry_ref[...] = u[ts - 8:ts, :]
    cst_ref[...] = u[ts - 2:ts, :]
    q_ref[...] = (q * (HD_F ** -0.5 * LOG2E)).astype(BF16)
    k_ref[...] = k
    v_ref[...] = v
    kb_ref[...] = k.astype(BF16)
    vb_ref[...] = v.astype(BF16)
    vt_ref[...] = v.T.astype(BF16)


def _inproj(x, g, w_main, w_fl, b_f, conv_prev, conv_w):
    B, S, D = x.shape
    ts = _tile(S, 512)
    tok = lambda w: pl.BlockSpec((None, ts, w), lambda b, s: (b, s, 0))
    full = lambda a: pl.BlockSpec(a.shape, lambda b, s: (0,) * a.ndim)
    st = pl.BlockSpec((None, 2, D_A), lambda b, s: (b, 0, 0))
    out_shape = (
        jax.ShapeDtypeStruct((B, S, D_A), BF16),
        jax.ShapeDtypeStruct((B, S, D_F), BF16),
        jax.ShapeDtypeStruct((B, S, D_F), F32),
        jax.ShapeDtypeStruct((B, S, D_F), F32),
        jax.ShapeDtypeStruct((B, S, D_F), BF16),
        jax.ShapeDtypeStruct((B, S, D_F), BF16),
        jax.ShapeDtypeStruct((B, D_F, S), BF16),
        jax.ShapeDtypeStruct((B, S, LANES), F32),
        jax.ShapeDtypeStruct((B, 2, D_A), F32),
    )
    vt_spec = pl.BlockSpec((None, D_F, ts), lambda b, s: (b, 0, s))
    return pl.pallas_call(
        _inproj_kernel,
        grid=(B, S // ts),
        in_specs=[tok(D), full(g), full(w_main), full(w_fl), full(b_f), st, full(conv_w)],
        out_specs=(tok(D_A), tok(D_F), tok(D_F), tok(D_F), tok(D_F), tok(D_F), vt_spec, tok(LANES), st),
        out_shape=out_shape,
        scratch_shapes=[pltpu.VMEM((8, D_A), F32)],
        compiler_params=_cparams("parallel", "arbitrary"),
        name="l0_inproj_conv",
    )(x, g, w_main, w_fl, b_f, conv_prev, conv_w)


def _scan_rows(x):
    n = x.shape[0]
    row = lax.broadcasted_iota(jnp.int32, x.shape, 0)
    sh = 1
    while sh < n:
        x = x + jnp.where(row >= sh, pltpu.roll(x, sh, 0), 0.0)
        sh *= 2
    return x


def _cumsum_kernel(x_ref, o_ref):
    o_ref[...] = _scan_rows(x_ref[...]) * LOG2E


def _cumsum_lanes_kernel(x_ref, o_ref):
    x = x_ref[...]
    n = x.shape[1]
    col = lax.broadcasted_iota(jnp.int32, x.shape, 1)
    sh = 1
    while sh < n:
        x = x + jnp.where(col >= sh, pltpu.roll(x, sh, 1), 0.0)
        sh *= 2
    o_ref[...] = (x - x[:, n - 1:n]) * LOG2E


def _cumsum_lanes(x):
    spec = pl.BlockSpec(x.shape, lambda i: (0, 0))
    return pl.pallas_call(
        _cumsum_lanes_kernel, grid=(1,), in_specs=[spec], out_specs=spec,
        out_shape=jax.ShapeDtypeStruct(x.shape, F32),
        compiler_params=_cparams("arbitrary"), name="cache_logf_cumsum",
    )(x)


def _cumsum_carrier_kernel(x_ref, pq_ref, pk_ref, oq_ref, ok_ref, o_ref, qc_ref, kc_ref):
    c = _scan_rows(x_ref[...]) * LOG2E
    o_ref[...] = c
    parts = jnp.concatenate(_split3(c), axis=1)
    qc_ref[...] = (jnp.dot(parts, pq_ref[...], preferred_element_type=F32) + oq_ref[...]).astype(BF16)
    kc_ref[...] = (jnp.dot(parts, pk_ref[...], preferred_element_type=F32) + ok_ref[...]).astype(BF16)


def _carrier_lane(h, slot):
    return LANES * (h // 2) + (HD_F if h % 2 == 0 else 0) + slot


def _carrier_placement():
    pq = np.zeros((3 * LANES, D_F), np.float32)
    pk = np.zeros((3 * LANES, D_F), np.float32)
    oq = np.zeros((1, D_F), np.float32)
    ok = np.zeros((1, D_F), np.float32)
    for h in range(H_F):
        for part in range(3):
            pq[part * LANES + h, _carrier_lane(h, part)] = 1.0
            ok[0, _carrier_lane(h, part)] = 1.0
            pk[part * LANES + h, _carrier_lane(h, 3 + part)] = -1.0
            oq[0, _carrier_lane(h, 3 + part)] = 1.0
    return jnp.asarray(pq, BF16), jnp.asarray(pk, BF16), jnp.asarray(oq), jnp.asarray(ok)


def _cumsum(x, carriers):
    B, S, W = x.shape
    spec = pl.BlockSpec((None, S, W), lambda b: (b, 0, 0))
    if not carriers:
        return pl.pallas_call(
            _cumsum_kernel, grid=(B,), in_specs=[spec], out_specs=spec,
            out_shape=jax.ShapeDtypeStruct(x.shape, F32),
            compiler_params=_cparams("parallel"), name="logf_cumsum",
        )(x)
    consts = _carrier_placement()
    cspec = pl.BlockSpec((None, S, D_F), lambda b: (b, 0, 0))
    return pl.pallas_call(
        _cumsum_carrier_kernel, grid=(B,),
        in_specs=[spec] + [pl.BlockSpec(a.shape, lambda b: (0, 0)) for a in consts],
        out_specs=(spec, cspec, cspec),
        out_shape=(jax.ShapeDtypeStruct(x.shape, F32), jax.ShapeDtypeStruct((B, S, D_F), BF16),
                   jax.ShapeDtypeStruct((B, S, D_F), BF16)),
        compiler_params=_cparams("parallel"), name="logf_cumsum_carriers",
    )(x, *consts)


def _fox_prompt_kernel(q_ref, k_ref, vt_ref, qc_ref, kc_ref, o_ref, m_ref, acc_ref):
    i = pl.program_id(1)
    j = pl.program_id(2)
    tq, tk = q_ref.shape[0], k_ref.shape[0]
    tqs = min(tq, ATTN_Q_SUB)

    @pl.when(j == 0)
    def _():
        m_ref[...] = jnp.full(m_ref.shape, NEG_INF, F32)
        acc_ref[...] = jnp.zeros(acc_ref.shape, F32)

    def update(diag):
        lane_q = lax.broadcasted_iota(jnp.int32, (tq, LANES), 1)
        lane_k = lax.broadcasted_iota(jnp.int32, (tk, LANES), 1)
        row_v = lax.broadcasted_iota(jnp.int32, (LANES, tk), 0)
        for pair in range(H_F // 2):
            sl = slice(pair * LANES, (pair + 1) * LANES)
            q2, qc2, k2, kc2, vt2 = q_ref[:, sl], qc_ref[:, sl], k_ref[:, sl], kc_ref[:, sl], vt_ref[sl, :]
            own = lambda idx, e: (idx < HD_F) == (e == 0)
            qa = [jnp.where(own(lane_q, e), q2, qc2) for e in range(2)]
            ka = [jnp.where(own(lane_k, e), k2, kc2) for e in range(2)]
            va = [jnp.where(own(row_v, e), vt2, jnp.ones_like(vt2)) for e in range(2)]
            units = [(e, qs) for e in range(2) for qs in range(tq // tqs)]
            cs = [slice(qs * tqs, (qs + 1) * tqs) for _, qs in units]
            nk = [(qs + 1) * tqs if diag else tk for _, qs in units]
            st = [_dot_nt(ka[e][:nk[u]], qa[e][cs[u]]) for u, (e, _) in enumerate(units)]
            if diag:
                for u, (_, qs) in enumerate(units):
                    key = lax.broadcasted_iota(jnp.int32, (nk[u], tqs), 0)
                    qry = lax.broadcasted_iota(jnp.int32, (nk[u], tqs), 1) + qs * tqs
                    st[u] = jnp.where(key <= qry, st[u], NEG_INF)
            m_old = [m_ref[2 * pair + e, :, cs[u]] for u, (e, _) in enumerate(units)]
            m_new = [jnp.maximum(m_old[u], jnp.max(st[u], axis=0, keepdims=True)) for u in range(len(units))]
            pt = [jnp.exp2(st[u] - m_new[u]).astype(BF16) for u in range(len(units))]
            pv = [jnp.dot(va[e][:, :nk[u]], pt[u], preferred_element_type=F32) for u, (e, _) in enumerate(units)]
            for u, (e, _) in enumerate(units):
                h = 2 * pair + e
                acc_ref[h, :, cs[u]] = jnp.exp2(m_old[u] - m_new[u]) * acc_ref[h, :, cs[u]] + pv[u]
                m_ref[h, :, cs[u]] = m_new[u]

    @pl.when(j < i)
    def _():
        update(False)

    @pl.when(j == i)
    def _():
        update(True)
        for pair in range(H_F // 2):
            a, b = acc_ref[2 * pair], acc_ref[2 * pair + 1]
            ot = jnp.concatenate([a[:HD_F] / a[HD_F:HD_F + 1], b[HD_F:] / b[0:1]], axis=0)
            o_ref[:, pair * LANES:(pair + 1) * LANES] = ot.T.astype(o_ref.dtype)


def _fox_prompt(q, kb, vt, qc, kc):
    B, S, _ = q.shape
    t = _tile(S, ATTN_TILE)
    n = S // t
    qs = pl.BlockSpec((None, t, D_F), lambda b, i, j: (b, i, 0))
    ks = pl.BlockSpec((None, t, D_F), lambda b, i, j: (b, jnp.minimum(j, i), 0))
    vs = pl.BlockSpec((None, D_F, t), lambda b, i, j: (b, 0, jnp.minimum(j, i)))
    return pl.pallas_call(
        _fox_prompt_kernel, grid=(B, n, n),
        in_specs=[qs, ks, vs, qs, ks], out_specs=qs,
        out_shape=jax.ShapeDtypeStruct((B, S, D_F), BF16),
        scratch_shapes=[pltpu.VMEM((H_F, 1, t), F32), pltpu.VMEM((H_F, LANES, t), F32)],
        compiler_params=_cparams("parallel", "parallel", "arbitrary"),
        name="fox_prompt_attention",
    )(q, kb, vt, qc, kc)


def _fox_sample_kernel(q_ref, ckt_ref, cvt_ref, dt_ref, kn_ref, vn_ref, cn_ref, cnt_ref, o_ref,
                       m_ref, l_ref, acc_ref):
    j = pl.program_id(2)
    T = q_ref.shape[0]

    @pl.when(j == 0)
    def _():
        m_ref[...] = jnp.full(m_ref.shape, NEG_INF, F32)
        l_ref[...] = jnp.zeros(l_ref.shape, F32)
        acc_ref[...] = jnp.zeros(acc_ref.shape, F32)

    q = q_ref[...]
    cn = cn_ref[...]

    def step(s, pv, mask):
        if mask is not None:
            s = jnp.where(mask, s, NEG_INF)
        m_old = m_ref[...]
        m_new = jnp.maximum(m_old, jnp.max(s, axis=-1, keepdims=True))
        alpha = jnp.exp2(m_old - m_new)
        p = jnp.exp2(s - m_new)
        l_ref[...] = alpha * l_ref[...] + jnp.sum(p, axis=-1, keepdims=True)
        acc_ref[...] = alpha * acc_ref[...] + pv(p.astype(BF16))
        m_ref[...] = m_new

    kt = ckt_ref[...].astype(BF16)
    vt = cvt_ref[...].astype(BF16)
    step(jnp.dot(q, kt, preferred_element_type=F32) + (cn - dt_ref[...]), lambda p: _dot_nt(p, vt), None)

    @pl.when(j == pl.num_programs(2) - 1)
    def _():
        rq = lax.broadcasted_iota(jnp.int32, (T, T), 0)
        rk = lax.broadcasted_iota(jnp.int32, (T, T), 1)
        vn = vn_ref[...]
        step(_dot_nt(q, kn_ref[...]) + (cn - cnt_ref[...]),
             lambda p: jnp.dot(p, vn, preferred_element_type=F32), rq >= rk)
        o_ref[...] = (acc_ref[...] / l_ref[...]).astype(o_ref.dtype)


def _fox_sample(q, ckt, cvt, dt, kb, vb, cn, cnt):
    B, T, _ = q.shape
    P = ckt.shape[-1]
    tk = _tile(P, 4096)
    heads = lambda a: jnp.swapaxes(a.reshape(B, T, H_F, HD_F), 1, 2)
    new = lambda w: pl.BlockSpec((None, None, T, w), lambda b, h, j: (b, h, 0, 0))
    cache = pl.BlockSpec((None, None, HD_F, tk), lambda b, h, j: (b, h, 0, j))
    o = pl.pallas_call(
        _fox_sample_kernel, grid=(B, H_F, P // tk),
        in_specs=[new(HD_F), cache, cache, pl.BlockSpec((None, None, 1, tk), lambda b, h, j: (b, h, 0, j)),
                  new(HD_F), new(HD_F), new(1), pl.BlockSpec((None, None, 1, T), lambda b, h, j: (b, h, 0, 0))],
        out_specs=new(HD_F),
        out_shape=jax.ShapeDtypeStruct((B, H_F, T, HD_F), BF16),
        scratch_shapes=[pltpu.VMEM((T, 1), F32), pltpu.VMEM((T, 1), F32), pltpu.VMEM((T, HD_F), F32)],
        compiler_params=_cparams("parallel", "parallel", "arbitrary"),
        name="fox_sample_attention",
    )(heads(q), ckt, cvt, dt[:, :, None, :], heads(kb), heads(vb),
      jnp.swapaxes(cn[:, :, :H_F], 1, 2)[..., None], cnt[:, :, None, :])
    return jnp.swapaxes(o, 1, 2).reshape(B, T, D_F)


def _channel_tail(h, p_ref, gpre_ref, wg_ref, wu_ref, wd_ref, gpost_ref, gple_ref, wpg_ref, wpp_ref,
                  out_ref, act_ref):
    xn = _rms(h, gpre_ref[...]).astype(BF16)
    for c in range(0, D_FF, FFN_CHUNK):
        gate = jnp.dot(xn, wg_ref[:, c:c + FFN_CHUNK], preferred_element_type=F32)
        up = jnp.dot(xn, wu_ref[:, c:c + FFN_CHUNK], preferred_element_type=F32)
        act_ref[:, c:c + FFN_CHUNK] = (gate * jax.nn.sigmoid(gate) * up).astype(BF16)
    h = h + _rms(jnp.dot(act_ref[...], wd_ref[...], preferred_element_type=F32), gpost_ref[...])
    gate = jax.nn.sigmoid(_bdot(_rms(h, gple_ref[...]), wpg_ref[...]))
    out_ref[...] = h + gate * _bdot(p_ref[...], wpp_ref[...])


def _l0_tail_kernel(ya_ref, o_ref, x_ref, p_ref, wout_ref, gmix_ref, *tail):
    y = (jnp.dot(ya_ref[...], wout_ref[0:D_A, :], preferred_element_type=F32)
         + jnp.dot(o_ref[...], wout_ref[D_A:D_A + D_F, :], preferred_element_type=F32))
    _channel_tail(x_ref[...] + _rms(y, gmix_ref[...]), p_ref, *tail)


def _l1_tail_kernel(o_ref, r_ref, k_ref, v_ref, g_ref, h_ref, p_ref, lnw_ref, lnb_ref, rk_ref, bd_ref,
                    wo_ref, gmix_ref, *tail):
    bd = bd_ref[...]
    o = o_ref[...]
    d = o - _head_allsum(o, bd) * (1.0 / HD_R)
    var = _head_allsum(d * d, bd) * (1.0 / HD_R)
    on = d * lax.rsqrt(var + GN_EPS) * lnw_ref[...] + lnb_ref[...]
    rk = r_ref[...].astype(F32) * k_ref[...].astype(F32) * rk_ref[...]
    bonus = _head_allsum(rk, bd) * v_ref[...].astype(F32)
    y = _bdot((on + bonus) * g_ref[...].astype(F32), wo_ref[...])
    _channel_tail(h_ref[...] + _rms(y, gmix_ref[...]), p_ref, *tail)


def _tail_call(kernel, tokens, consts, name):
    n = tokens[0].shape[0]
    tm = _tile(n, 512)
    tok = lambda w: pl.BlockSpec((tm, w), lambda i: (i, 0))
    res = lambda a: pl.BlockSpec(a.shape, lambda i: (0,) * a.ndim, pipeline_mode=pl.Buffered(1))
    return pl.pallas_call(
        kernel, grid=(n // tm,),
        in_specs=[tok(t.shape[1]) for t in tokens] + [res(c) for c in consts],
        out_specs=tok(D_MODEL),
        out_shape=jax.ShapeDtypeStruct((n, D_MODEL), F32),
        scratch_shapes=[pltpu.VMEM((tm, D_FF), BF16)],
        compiler_params=_cparams("parallel"), name=name,
    )(*tokens, *consts)


def _rwkv_prep_kernel(h_ref, sprev_ref, gpre_ref, mu_ref, wr_ref, wk_ref, wv_ref, w0_ref, w1_ref, w2_ref,
                      a0_ref, a1_ref, a2_ref, g1_ref, g2_ref, kk_ref, ka_ref, bd_ref,
                      r_out, lw_out, k_out, v_out, kk_out, b_out, g_out, shift_out, carry_ref):
    ts = h_ref.shape[0]

    @pl.when(pl.program_id(1) == 0)
    def _():
        carry_ref[7:8, :] = sprev_ref[...]

    xn = _rms(h_ref[...], gpre_ref[...])
    row = lax.broadcasted_iota(jnp.int32, xn.shape, 0)
    xx = jnp.where(row == 0, carry_ref[7:8, :], pltpu.roll(xn, 1, 0)) - xn
    carry_ref[...] = xn[ts - 8:ts, :]
    shift_out[...] = xn[ts - 1:ts, :]
    xnb, xxb, mub = xn.astype(BF16), xx.astype(BF16), mu_ref[...].astype(BF16)
    mix = lambda n: xnb + xxb * mub[n:n + 1, :]
    r = _bdot(mix(0), wr_ref[...])
    k = _bdot(mix(2), wk_ref[...])
    v = _bdot(mix(3), wv_ref[...])
    wl = w0_ref[...] + _bdot(jnp.tanh(_bdot(mix(1), w1_ref[...])), w2_ref[...])
    a = jax.nn.sigmoid(a0_ref[...] + _bdot(_bdot(mix(4), a1_ref[...]), a2_ref[...]))
    g = _bdot(jax.nn.sigmoid(_bdot(mix(5), g1_ref[...])), g2_ref[...])
    kkr = k * kk_ref[...]
    kk = kkr * lax.rsqrt(jnp.maximum(_head_allsum(kkr * kkr, bd_ref[...]), L2_EPS * L2_EPS))
    r_out[...] = r.astype(BF16)
    lw_out[...] = -DECAY_SCALE * jax.nn.sigmoid(wl)
    k_out[...] = (k * (1.0 + (a - 1.0) * ka_ref[...])).astype(BF16)
    v_out[...] = v.astype(BF16)
    kk_out[...] = kk.astype(BF16)
    b_out[...] = (kk * a).astype(BF16)
    g_out[...] = g.astype(BF16)


def _rwkv_prep(h, shift_prev, consts):
    B, S, D = h.shape
    ts = _tile(S, 512)
    tok = pl.BlockSpec((None, ts, D), lambda b, s: (b, s, 0))
    row = pl.BlockSpec((None, 1, D), lambda b, s: (b, 0, 0))
    full = lambda a: pl.BlockSpec(a.shape, lambda b, s: (0,) * a.ndim)
    big = lambda dt: jax.ShapeDtypeStruct((B, S, D), dt)
    return pl.pallas_call(
        _rwkv_prep_kernel, grid=(B, S // ts),
        in_specs=[tok, row] + [full(c) for c in consts],
        out_specs=(tok,) * 7 + (row,),
        out_shape=(big(BF16), big(F32)) + (big(BF16),) * 5 + (jax.ShapeDtypeStruct((B, 1, D), F32),),
        scratch_shapes=[pltpu.VMEM((8, D), F32)],
        compiler_params=_cparams("parallel", "arbitrary"), name="rwkv_prep",
    )(h, shift_prev, *consts)


def _wkv_kernel(r_ref, lw_ref, k_ref, v_ref, kk_ref, b_ref, s0_ref, o_ref, s_ref, sbd_ref):
    c = pl.program_id(1)
    nb, L = r_ref.shape[0], r_ref.shape[1]
    W = 2 * HD_R
    assert L == HD_R
    n_pairs = H_R // 2
    mm = functools.partial(jnp.dot, preferred_element_type=F32)
    units = [(bi, p) for bi in range(nb) for p in range(n_pairs)]

    @pl.when(c == 0)
    def _():
        zero = jnp.zeros((HD_R, HD_R), F32)
        for u, (bi, p) in enumerate(units):
            sbd_ref[u] = jnp.concatenate([jnp.concatenate([s0_ref[bi, 2 * p], zero], axis=1),
                                          jnp.concatenate([zero, s0_ref[bi, 2 * p + 1]], axis=1)], axis=0)

    at, rt, bt, kt, w_end = [], [], [], [], []
    for bi in range(nb):
        lw = lw_ref[bi]
        row = lax.broadcasted_iota(jnp.int32, lw.shape, 0)
        cum = lw
        sh = 1
        while sh < L:
            cum = cum + jnp.where(row >= sh, pltpu.roll(cum, sh, 0), 0.0)
            sh *= 2
        w_inv = jnp.exp(-cum)
        at.append((-kk_ref[bi].astype(F32) * jnp.exp(cum - lw)).astype(BF16))
        rt.append((r_ref[bi].astype(F32) * jnp.exp(cum)).astype(BF16))
        bt.append((b_ref[bi].astype(F32) * w_inv).astype(BF16))
        kt.append((k_ref[bi].astype(F32) * w_inv).astype(BF16))
        w_end.append(jnp.exp(cum[L - 1:L, :]))

    ri = lax.broadcasted_iota(jnp.int32, (2 * L, W), 0)
    ci = lax.broadcasted_iota(jnp.int32, (2 * L, W), 1)
    causal = (ri % L + ri // L) > ci % L
    lo2 = ci < HD_R
    r1 = lax.broadcasted_iota(jnp.int32, (L, W), 0)
    c1 = lax.broadcasted_iota(jnp.int32, (L, W), 1)
    lo1 = c1 < HD_R
    same_head = (ri // HD_R) == (ci // HD_R)
    eye2 = (r1 == c1 % L).astype(F32)

    def blockdiag(x):
        return jnp.concatenate([jnp.where(lo1, x, jnp.zeros_like(x)), jnp.where(lo1, jnp.zeros_like(x), x)], axis=0)

    pairs = range(len(units))
    sl = [slice(p * W, (p + 1) * W) for _, p in units]
    ar = [jnp.concatenate([at[bi][:, sl[u]], rt[bi][:, sl[u]]], axis=0) for u, (bi, _) in enumerate(units)]
    bk = [jnp.concatenate([bt[bi][:, sl[u]], kt[bi][:, sl[u]]], axis=0) for u, (bi, _) in enumerate(units)]
    kb = [jnp.concatenate([kt[bi][:, sl[u]], bt[bi][:, sl[u]]], axis=0) for u, (bi, _) in enumerate(units)]
    v2 = [v_ref[bi, :, sl[u]] for u, (bi, _) in enumerate(units)]
    S2 = [sbd_ref[u] for u in pairs]
    zeros2 = jnp.zeros((2 * L, W), BF16)
    g_a = [jnp.where(causal, _dot_nt(jnp.where(lo2, ar[p], zeros2), bk[p]), 0.0) for p in pairs]
    g_b = [jnp.where(causal, _dot_nt(jnp.where(lo2, zeros2, ar[p]), kb[p]), 0.0) for p in pairs]
    gb2 = [jnp.where(lo2, g_a[p], g_b[p]) for p in pairs]
    gk2 = [jnp.where(lo2, g_b[p], g_a[p]).astype(BF16) for p in pairs]
    zv = jnp.zeros((L, W), BF16)
    vx = [jnp.concatenate([jnp.where(lo1, zv, v2[p]), jnp.where(lo1, v2[p], zv)], axis=0) for p in pairs]
    xs = [_dot_nt(ar[p], S2[p].astype(BF16)) + mm(gk2[p], vx[p]) for p in pairs]
    inv = [eye2 + g[:L] for g in gb2]
    pw = [g[:L].astype(BF16) for g in gb2]
    pw = [mm(q, blockdiag(q)) for q in pw]
    sq = 2
    while sq < L:
        pwb = [q.astype(BF16) for q in pw]
        both = [mm(jnp.concatenate([inv[p].astype(BF16), pwb[p]], axis=0), blockdiag(pwb[p])) for p in pairs]
        inv = [inv[p] + both[p][:L] for p in pairs]
        pw = [b_[L:] for b_ in both]
        sq *= 2
    ub = [mm(inv[p].astype(BF16), blockdiag(xs[p][:L].astype(BF16))).astype(BF16) for p in pairs]
    for u, (bi, _) in enumerate(units):
        o_ref[bi, :, sl[u]] = xs[u][L:] + mm(gb2[u][L:].astype(BF16), blockdiag(ub[u]))
    for u, (bi, _) in enumerate(units):
        upd = _dot_tn(jnp.concatenate([ub[u], v2[u]], axis=0), bk[u])
        sbd_ref[u] = (S2[u] + jnp.where(same_head, upd, 0.0)) * w_end[bi][:, sl[u]]

    @pl.when(c == pl.num_programs(1) - 1)
    def _():
        for u, (bi, p) in enumerate(units):
            s_ref[bi, 2 * p] = sbd_ref[u, 0:HD_R, 0:HD_R]
            s_ref[bi, 2 * p + 1] = sbd_ref[u, HD_R:W, HD_R:W]


def _wkv(r, lw, k, v, kk, b, s0):
    B, S, D = r.shape
    L = _tile(S, WKV_CHUNK)
    nb = _tile(B, WKV_SEQS)
    tok = pl.BlockSpec((nb, L, D), lambda b_, c: (b_, c, 0))
    st = pl.BlockSpec((nb, H_R, HD_R, HD_R), lambda b_, c: (b_, 0, 0, 0))
    return pl.pallas_call(
        _wkv_kernel, grid=(B // nb, S // L),
        in_specs=[tok] * 6 + [st], out_specs=(tok, st),
        out_shape=(jax.ShapeDtypeStruct((B, S, D), F32), jax.ShapeDtypeStruct(s0.shape, F32)),
        scratch_shapes=[pltpu.VMEM((nb * H_R // 2, 2 * HD_R, 2 * HD_R), F32)],
        compiler_params=_cparams("parallel", "arbitrary"), name="wkv_chunked",
    )(r, lw, k, v, kk, b, s0)


def _prep_weights(W):
    bf = lambda a: a.astype(BF16)
    row = lambda a: a.reshape(1, -1)
    P = dict(W)
    w_in = W["e_w_in"][0]
    n_main = 3 * D_A + 3 * D_F
    P["w_main"] = bf(w_in[:, :n_main])
    P["w_fl"] = bf(jnp.pad(w_in[:, n_main:], ((0, 0), (0, LANES - H_F))))
    P["b_f"] = jnp.pad(W["e_b_f"][0], (0, LANES - H_F)).reshape(1, LANES)
    P["w_out"] = bf(W["e_w_out"][0])
    for n in ("f_w_gate", "f_w_up", "f_w_down", "ple_gate", "ple_proj"):
        P[n] = bf(W[n])
    for n in ("r_w_r", "r_w_k", "r_w_v", "r_w_o", "r_w1", "r_w2", "r_a1", "r_a2", "r_g1", "r_g2"):
        P[n] = bf(W[n][0])
    for n in ("r_w0", "r_a0", "r_k_k", "r_k_a", "r_ln_w", "r_ln_b"):
        P[n] = row(W[n][0])
    P["r_r_k"] = W["r_r_k"][0].reshape(1, D_MODEL)
    P["r_mu"] = W["r_mu"][0]
    P["bd"] = _head_blockdiag(HD_R)
    return P


def _trunk(x, p, fox_cache, conv_prev, shift_prev, wkv_prev, P):
    B, S, D = x.shape
    n = B * S
    flat = lambda a: a.reshape(n, a.shape[-1])
    vec = lambda name, i: P[name][i].reshape(1, D)

    ya, q, k, v, kb, vb, vt, lf, cst = _inproj(x, vec("mix_norm_pre", 0), P["w_main"], P["w_fl"], P["b_f"],
                                              conv_prev, P["e_conv_w"][0])
    if fox_cache is None:
        _, qc, kc = _cumsum(lf, carriers=True)
        o = _fox_prompt(q, kb, vt, qc, kc)
    else:
        ck, cv, clf = fox_cache
        Pn = ck.shape[1]
        c = _cumsum(lf, carriers=False)
        ct = jnp.swapaxes(c[:, :, :8], 1, 2)
        dt = _cumsum_lanes(jnp.swapaxes(clf, 1, 2).reshape(B * H_F, Pn)).reshape(B, H_F, Pn)
        to_t = lambda a: jnp.transpose(a, (0, 2, 3, 1))
        o = _fox_sample(q, to_t(ck), to_t(cv), dt, kb, vb, c, ct)
    tail_consts = lambda i: [vec("ffn_norm_pre", i), P["f_w_gate"][i], P["f_w_up"][i], P["f_w_down"][i],
                             vec("ffn_norm_post", i), vec("ple_norm", i), P["ple_gate"][i], P["ple_proj"][i]]
    h = _tail_call(_l0_tail_kernel, [flat(ya), flat(o), flat(x), flat(p[0])],
                   [P["w_out"], vec("mix_norm_post", 0)] + tail_consts(0), "l0_outproj_ffn_ple")

    consts = [vec("mix_norm_pre", 1), P["r_mu"], P["r_w_r"], P["r_w_k"], P["r_w_v"], P["r_w0"], P["r_w1"], P["r_w2"],
              P["r_a0"], P["r_a1"], P["r_a2"], P["r_g1"], P["r_g2"], P["r_k_k"], P["r_k_a"], P["bd"]]
    r, lw, km, vv, kk, bb, g, shift = _rwkv_prep(h.reshape(B, S, D), shift_prev.reshape(B, 1, D), consts)
    o1, wkv = _wkv(r, lw, km, vv, kk, bb, wkv_prev)
    h = _tail_call(_l1_tail_kernel, [flat(o1), flat(r), flat(km), flat(vv), flat(g), h, flat(p[1])],
                   [P["r_ln_w"], P["r_ln_b"], P["r_r_k"], P["bd"], P["r_w_o"], vec("mix_norm_post", 1)]
                   + tail_consts(1), "l1_rwkvout_ffn_ple")

    return (h.reshape(B, S, D), k.reshape(1, B, S, H_F, HD_F), v.reshape(1, B, S, H_F, HD_F),
            lf[None, :, :, :H_F], cst[None], shift.reshape(1, B, D), wkv[None])


def kernel(x_prompt, x_sample, p_prompt, p_sample, cache_k, cache_v, cache_logf, state_conv, state_shift, state_wkv, mix_norm_pre, mix_norm_post, ffn_norm_pre, ffn_norm_post, e_w_in, e_b_f, e_conv_w, e_w_out, r_mu, r_w_r, r_w_k, r_w_v, r_w_o, r_w0, r_w1, r_w2, r_a0, r_a1, r_a2, r_g1, r_g2, r_k_k, r_k_a, r_r_k, r_ln_w, r_ln_b, f_w_gate, f_w_up, f_w_down, ple_norm, ple_gate, ple_proj):
    W = dict(mix_norm_pre=mix_norm_pre, mix_norm_post=mix_norm_post, ffn_norm_pre=ffn_norm_pre,
             ffn_norm_post=ffn_norm_post, e_w_in=e_w_in, e_b_f=e_b_f, e_conv_w=e_conv_w, e_w_out=e_w_out,
             r_mu=r_mu, r_w_r=r_w_r, r_w_k=r_w_k, r_w_v=r_w_v, r_w_o=r_w_o, r_w0=r_w0, r_w1=r_w1, r_w2=r_w2,
             r_a0=r_a0, r_a1=r_a1, r_a2=r_a2, r_g1=r_g1, r_g2=r_g2, r_k_k=r_k_k, r_k_a=r_k_a, r_r_k=r_r_k,
             r_ln_w=r_ln_w, r_ln_b=r_ln_b, f_w_gate=f_w_gate, f_w_up=f_w_up, f_w_down=f_w_down,
             ple_norm=ple_norm, ple_gate=ple_gate, ple_proj=ple_proj)
    P = _prep_weights(W)
    bp = x_prompt.shape[0]
    y_p, k_p, v_p, lf_p, c_p, sh_p, s_p = _trunk(
        x_prompt, p_prompt, None, jnp.zeros((bp, 2, D_A), F32), jnp.zeros((bp, D_MODEL), F32),
        jnp.zeros((bp, H_R, HD_R, HD_R), F32), P)
    y_s, k_s, v_s, lf_s, c_s, sh_s, s_s = _trunk(
        x_sample, p_sample, (cache_k[0], cache_v[0], cache_logf[0]), state_conv[0], state_shift[0], state_wkv[0], P)
    return (y_p, y_s, k_p, v_p, lf_p, c_p, sh_p, s_p, k_s, v_s, lf_s, c_s, sh_s, s_s)
```

```python
import functools

import jax
import jax.numpy as jnp
import numpy as np
from jax import lax
from jax.experimental import pallas as pl
from jax.experimental.pallas import tpu as pltpu

D_MODEL = 1024
D_A = 512
H_F = 8
HD_F = 64
D_F = H_F * HD_F
HD_R = 64
H_R = D_MODEL // HD_R
PLE_DIM = 256
D_FF = 2816
NORM_EPS = 1e-6
GN_EPS = 64e-5
L2_EPS = 1e-12
NEG_INF = -1e30

LOG2E = 1.4426950408889634
DECAY_SCALE = 0.6065306597126334
LANES = 128
FFN_CHUNK = 256
SAMPLE_HEADS = 2
ATTN_TILE = 1024
ATTN_Q_SUB = 256
WKV_SEQS = 2
WKV_CHUNK = 64
VMEM_LIMIT = 48 * 1024 * 1024

BF16 = jnp.bfloat16
F32 = jnp.float32


def _cparams(*sem):
    return pltpu.CompilerParams(dimension_semantics=sem, vmem_limit_bytes=VMEM_LIMIT)


def _tile(n, pref):
    t = min(n, pref)
    assert n % t == 0, (n, pref)
    return t


def _rms(x, g):
    return x * lax.rsqrt(jnp.mean(x * x, axis=-1, keepdims=True) + NORM_EPS) * g


def _bdot(a, w):
    return jnp.dot(a.astype(BF16), w, preferred_element_type=F32)


def _dot_nt(a, b, **kw):
    return lax.dot_general(a, b, (((1,), (1,)), ((), ())), preferred_element_type=F32, **kw)


def _dot_tn(a, b, **kw):
    return lax.dot_general(a, b, (((0,), (0,)), ((), ())), preferred_element_type=F32, **kw)


def _softplus(y):
    return jnp.maximum(y, 0.0) + jnp.log1p(jnp.exp(-jnp.abs(y)))


def _split3(x):
    hi = x.astype(BF16)
    r1 = x - hi.astype(F32)
    mid = r1.astype(BF16)
    lo = (r1 - mid.astype(F32)).astype(BF16)
    return hi, mid, lo


def _head_allsum(x, bd, exact=True):
    hi = x.astype(BF16)
    w = bd.shape[0]
    slabs = [jnp.dot(hi[:, t:t + w], bd, preferred_element_type=F32) for t in range(0, x.shape[1], w)]
    if exact:
        lo = (x - hi.astype(F32)).astype(BF16)
        slabs = [sb + jnp.dot(lo[:, t:t + w], bd, preferred_element_type=F32)
                 for sb, t in zip(slabs, range(0, x.shape[1], w))]
    return jnp.concatenate(slabs, axis=1)


def _head_blockdiag(hd, width=256):
    idx = jnp.arange(width) // hd
    return (idx[:, None] == idx[None, :]).astype(BF16)


def _inproj_kernel(x_ref, g_ref, w_ref, wfl_ref, bf_ref, cprev_ref, cw_ref,
                   ya_ref, q_ref, k_ref, v_ref, kb_ref, vb_ref, vt_ref, lf_ref, cst_ref, carry_ref):
    ts = x_ref.shape[0]

    @pl.when(pl.program_id(1) == 0)
    def _():
        carry_ref[6:8, :] = cprev_ref[...]

    xn = _rms(x_ref[...], g_ref[...]).astype(BF16)
    z = [jnp.dot(xn, w_ref[:, c * D_A:(c + 1) * D_A], preferred_element_type=F32) for c in range(6)]
    ax, a_b, a_c, q, k, v = z
    fl = jnp.dot(xn, wfl_ref[...], preferred_element_type=F32) + bf_ref[...]
    lf_ref[...] = -_softplus(-fl)
    u = a_c * ax
    p0 = carry_ref[6:7, :]
    p1 = carry_ref[7:8, :]
    row = lax.broadcasted_iota(jnp.int32, u.shape, 0)
    um1 = jnp.where(row == 0, p1, pltpu.roll(u, 1, 0))
    um2 = jnp.where(row == 0, p0, jnp.where(row == 1, p1, pltpu.roll(u, 2, 0)))
    cu = cw_ref[0:1, :] * um2 + cw_ref[1:2, :] * um1 + cw_ref[2:3, :] * u
    ya_ref[...] = (a_b * cu).astype(BF16)
    carry_ref[...] = u[ts - 8:ts, :]
    cst_ref[...] = u[ts - 2:ts, :]
    q_ref[...] = (q * (HD_F ** -0.5 * LOG2E)).astype(BF16)
    k_ref[...] = k
    v_ref[...] = v
    kb_ref[...] = k.astype(BF16)
    vb_ref[...] = v.astype(BF16)
    vt_ref[...] = v.T.astype(BF16)


def _inproj(x, g, w_main, w_fl, b_f, conv_prev, conv_w):
    B, S, D = x.shape
    ts = _tile(S, 512)
    tok = lambda w: pl.BlockSpec((None, ts, w), lambda b, s: (b, s, 0))
    full = lambda a: pl.BlockSpec(a.shape, lambda b, s: (0,) * a.ndim)
    st = pl.BlockSpec((None, 2, D_A), lambda b, s: (b, 0, 0))
    out_shape = (
        jax.ShapeDtypeStruct((B, S, D_A), BF16),
        jax.ShapeDtypeStruct((B, S, D_F), BF16),
        jax.ShapeDtypeStruct((B, S, D_F), F32),
        jax.ShapeDtypeStruct((B, S, D_F), F32),
        jax.ShapeDtypeStruct((B, S, D_F), BF16),
        jax.ShapeDtypeStruct((B, S, D_F), BF16),
        jax.ShapeDtypeStruct((B, D_F, S), BF16),
        jax.ShapeDtypeStruct((B, S, LANES), F32),
        jax.ShapeDtypeStruct((B, 2, D_A), F32),
    )
    vt_spec = pl.BlockSpec((None, D_F, ts), lambda b, s: (b, 0, s))
    return pl.pallas_call(
        _inproj_kernel,
        grid=(B, S // ts),
        in_specs=[tok(D), full(g), full(w_main), full(w_fl), full(b_f), st, full(conv_w)],
        out_specs=(tok(D_A), tok(D_F), tok(D_F), tok(D_F), tok(D_F), tok(D_F), vt_spec, tok(LANES), st),
        out_shape=out_shape,
        scratch_shapes=[pltpu.VMEM((8, D_A), F32)],
        compiler_params=_cparams("parallel", "arbitrary"),
        name="l0_inproj_conv",
    )(x, g, w_main, w_fl, b_f, conv_prev, conv_w)


def _scan_rows(x):
    n = x.shape[0]
    row = lax.broadcasted_iota(jnp.int32, x.shape, 0)
    sh = 1
    while sh < n:
        x = x + jnp.where(row >= sh, pltpu.roll(x, sh, 0), 0.0)
        sh *= 2
    return x


def _cumsum_kernel(x_ref, o_ref):
    o_ref[...] = _scan_rows(x_ref[...]) * LOG2E


def _cumsum_lanes_kernel(x_ref, o_ref):
    x = x_ref[...]
    n = x.shape[1]
    col = lax.broadcasted_iota(jnp.int32, x.shape, 1)
    sh = 1
    while sh < n:
        x = x + jnp.where(col >= sh, pltpu.roll(x, sh, 1), 0.0)
        sh *= 2
    o_ref[...] = (x - x[:, n - 1:n]) * LOG2E


def _cumsum_lanes(x):
    spec = pl.BlockSpec(x.shape, lambda i: (0, 0))
    return pl.pallas_call(
        _cumsum_lanes_kernel, grid=(1,), in_specs=[spec], out_specs=spec,
        out_shape=jax.ShapeDtypeStruct(x.shape, F32),
        compiler_params=_cparams("arbitrary"), name="cache_logf_cumsum",
    )(x)


def _cumsum_carrier_kernel(x_ref, pq_ref, pk_ref, oq_ref, ok_ref, o_ref, qc_ref, kc_ref):
    c = _scan_rows(x_ref[...]) * LOG2E
    o_ref[...] = c
    parts = jnp.concatenate(_split3(c), axis=1)
    qc_ref[...] = (jnp.dot(parts, pq_ref[...], preferred_element_type=F32) + oq_ref[...]).astype(BF16)
    kc_ref[...] = (jnp.dot(parts, pk_ref[...], preferred_element_type=F32) + ok_ref[...]).astype(BF16)


def _carrier_lane(h, slot):
    return LANES * (h // 2) + (HD_F if h % 2 == 0 else 0) + slot


def _carrier_placement():
    pq = np.zeros((3 * LANES, D_F), np.float32)
    pk = np.zeros((3 * LANES, D_F), np.float32)
    oq = np.zeros((1, D_F), np.float32)
    ok = np.zeros((1, D_F), np.float32)
    for h in range(H_F):
        for part in range(3):
            pq[part * LANES + h, _carrier_lane(h, part)] = 1.0
            ok[0, _carrier_lane(h, part)] = 1.0
            pk[part * LANES + h, _carrier_lane(h, 3 + part)] = -1.0
            oq[0, _carrier_lane(h, 3 + part)] = 1.0
    return jnp.asarray(pq, BF16), jnp.asarray(pk, BF16), jnp.asarray(oq), jnp.asarray(ok)


def _cumsum(x, carriers):
    B, S, W = x.shape
    spec = pl.BlockSpec((None, S, W), lambda b: (b, 0, 0))
    if not carriers:
        return pl.pallas_call(
            _cumsum_kernel, grid=(B,), in_specs=[spec], out_specs=spec,
            out_shape=jax.ShapeDtypeStruct(x.shape, F32),
            compiler_params=_cparams("parallel"), name="logf_cumsum",
        )(x)
    consts = _carrier_placement()
    cspec = pl.BlockSpec((None, S, D_F), lambda b: (b, 0, 0))
    return pl.pallas_call(
        _cumsum_carrier_kernel, grid=(B,),
        in_specs=[spec] + [pl.BlockSpec(a.shape, lambda b: (0, 0)) for a in consts],
        out_specs=(spec, cspec, cspec),
        out_shape=(jax.ShapeDtypeStruct(x.shape, F32), jax.ShapeDtypeStruct((B, S, D_F), BF16),
                   jax.ShapeDtypeStruct((B, S, D_F), BF16)),
        compiler_params=_cparams("parallel"), name="logf_cumsum_carriers",
    )(x, *consts)


def _fox_prompt_kernel(q_ref, k_ref, vt_ref, qc_ref, kc_ref, o_ref, m_ref, acc_ref):
    i = pl.program_id(1)
    j = pl.program_id(2)
    tq, tk = q_ref.shape[0], k_ref.shape[0]
    tqs = min(tq, ATTN_Q_SUB)

    @pl.when(j == 0)
    def _():
        m_ref[...] = jnp.full(m_ref.shape, NEG_INF, F32)
        acc_ref[...] = jnp.zeros(acc_ref.shape, F32)

    def update(diag):
        lane_q = lax.broadcasted_iota(jnp.int32, (tq, LANES), 1)
        lane_k = lax.broadcasted_iota(jnp.int32, (tk, LANES), 1)
        row_v = lax.broadcasted_iota(jnp.int32, (LANES, tk), 0)
        for pair in range(H_F // 2):
            sl = slice(pair * LANES, (pair + 1) * LANES)
            q2, qc2, k2, kc2, vt2 = q_ref[:, sl], qc_ref[:, sl], k_ref[:, sl], kc_ref[:, sl], vt_ref[sl, :]
            own = lambda idx, e: (idx < HD_F) == (e == 0)
            qa = [jnp.where(own(lane_q, e), q2, qc2) for e in range(2)]
            ka = [jnp.where(own(lane_k, e), k2, kc2) for e in range(2)]
            va = [jnp.where(own(row_v, e), vt2, jnp.ones_like(vt2)) for e in range(2)]
            units = [(e, qs) for e in range(2) for qs in range(tq // tqs)]
            cs = [slice(qs * tqs, (qs + 1) * tqs) for _, qs in units]
            nk = [(qs + 1) * tqs if diag else tk for _, qs in units]
            st = [_dot_nt(ka[e][:nk[u]], qa[e][cs[u]]) for u, (e, _) in enumerate(units)]
            if diag:
                for u, (_, qs) in enumerate(units):
                    key = lax.broadcasted_iota(jnp.int32, (nk[u], tqs), 0)
                    qry = lax.broadcasted_iota(jnp.int32, (nk[u], tqs), 1) + qs * tqs
                    st[u] = jnp.where(key <= qry, st[u], NEG_INF)
            m_old = [m_ref[2 * pair + e, :, cs[u]] for u, (e, _) in enumerate(units)]
            m_new = [jnp.maximum(m_old[u], jnp.max(st[u], axis=0, keepdims=True)) for u in range(len(units))]
            pt = [jnp.exp2(st[u] - m_new[u]).astype(BF16) for u in range(len(units))]
            pv = [jnp.dot(va[e][:, :nk[u]], pt[u], preferred_element_type=F32) for u, (e, _) in enumerate(units)]
            for u, (e, _) in enumerate(units):
                h = 2 * pair + e
                acc_ref[h, :, cs[u]] = jnp.exp2(m_old[u] - m_new[u]) * acc_ref[h, :, cs[u]] + pv[u]
                m_ref[h, :, cs[u]] = m_new[u]

    @pl.when(j < i)
    def _():
        update(False)

    @pl.when(j == i)
    def _():
        update(True)
        for pair in range(H_F // 2):
            a, b = acc_ref[2 * pair], acc_ref[2 * pair + 1]
            ot = jnp.concatenate([a[:HD_F] / a[HD_F:HD_F + 1], b[HD_F:] / b[0:1]], axis=0)
            o_ref[:, pair * LANES:(pair + 1) * LANES] = ot.T.astype(o_ref.dtype)


def _fox_prompt(q, kb, vt, qc, kc):
    B, S, _ = q.shape
    t = _tile(S, ATTN_TILE)
    n = S // t
    qs = pl.BlockSpec((None, t, D_F), lambda b, i, j: (b, i, 0))
    ks = pl.BlockSpec((None, t, D_F), lambda b, i, j: (b, jnp.minimum(j, i), 0))
    vs = pl.BlockSpec((None, D_F, t), lambda b, i, j: (b, 0, jnp.minimum(j, i)))
    return pl.pallas_call(
        _fox_prompt_kernel, grid=(B, n, n),
        in_specs=[qs, ks, vs, qs, ks], out_specs=qs,
        out_shape=jax.ShapeDtypeStruct((B, S, D_F), BF16),
        scratch_shapes=[pltpu.VMEM((H_F, 1, t), F32), pltpu.VMEM((H_F, LANES, t), F32)],
        compiler_params=_cparams("parallel", "parallel", "arbitrary"),
        name="fox_prompt_attention",
    )(q, kb, vt, qc, kc)


def _fox_sample_kernel(q_ref, ckt_ref, cvt_ref, dt_ref, kn_ref, vn_ref, cn_ref, cnt_ref, o_ref,
                       m_ref, l_ref, acc_ref):
    j = pl.program_id(2)
    nh, T = q_ref.shape[0], q_ref.shape[1]
    heads = range(nh)

    @pl.when(j == 0)
    def _():
        m_ref[...] = jnp.full(m_ref.shape, NEG_INF, F32)
        l_ref[...] = jnp.zeros(l_ref.shape, F32)
        acc_ref[...] = jnp.zeros(acc_ref.shape, F32)

    def update(s, pv):
        m_old = [m_ref[h] for h in heads]
        m_new = [jnp.maximum(m_old[h], jnp.max(s[h], axis=-1, keepdims=True)) for h in heads]
        p = [jnp.exp2(s[h] - m_new[h]) for h in heads]
        o = [pv(h, p[h].astype(BF16)) for h in heads]
        for h in heads:
            alpha = jnp.exp2(m_old[h] - m_new[h])
            l_ref[h] = alpha * l_ref[h] + jnp.sum(p[h], axis=-1, keepdims=True)
            acc_ref[h] = alpha * acc_ref[h] + o[h]
            m_ref[h] = m_new[h]

    q = [q_ref[h] for h in heads]
    cn = [cn_ref[h] for h in heads]
    kt = [ckt_ref[h].astype(BF16) for h in heads]
    vt = [cvt_ref[h].astype(BF16) for h in heads]
    update([jnp.dot(q[h], kt[h], preferred_element_type=F32) + (cn[h] - dt_ref[h]) for h in heads],
           lambda h, p: _dot_nt(p, vt[h]))

    @pl.when(j == pl.num_programs(2) - 1)
    def _():
        causal = lax.broadcasted_iota(jnp.int32, (T, T), 0) >= lax.broadcasted_iota(jnp.int32, (T, T), 1)
        update([jnp.where(causal, _dot_nt(q[h], kn_ref[h]) + (cn[h] - cnt_ref[h]), NEG_INF) for h in heads],
               lambda h, p: jnp.dot(p, vn_ref[h], preferred_element_type=F32))
        for h in heads:
            o_ref[h] = (acc_ref[h] / l_ref[h]).astype(o_ref.dtype)


def _fox_sample(q, ckt, cvt, dt, kb, vb, cn, cnt):
    B, T, _ = q.shape
    P = ckt.shape[-1]
    tk = _tile(P, 4096)
    nh = SAMPLE_HEADS
    heads = lambda a: jnp.swapaxes(a.reshape(B, T, H_F, HD_F), 1, 2)
    new = lambda w: pl.BlockSpec((None, nh, T, w), lambda b, h, j: (b, h, 0, 0))
    cache = pl.BlockSpec((None, nh, HD_F, tk), lambda b, h, j: (b, h, 0, j))
    o = pl.pallas_call(
        _fox_sample_kernel, grid=(B, H_F // nh, P // tk),
        in_specs=[new(HD_F), cache, cache, pl.BlockSpec((None, nh, 1, tk), lambda b, h, j: (b, h, 0, j)),
                  new(HD_F), new(HD_F), new(1), pl.BlockSpec((None, nh, 1, T), lambda b, h, j: (b, h, 0, 0))],
        out_specs=new(HD_F),
        out_shape=jax.ShapeDtypeStruct((B, H_F, T, HD_F), BF16),
        scratch_shapes=[pltpu.VMEM((nh, T, 1), F32), pltpu.VMEM((nh, T, 1), F32), pltpu.VMEM((nh, T, HD_F), F32)],
        compiler_params=_cparams("parallel", "parallel", "arbitrary"),
        name="fox_sample_attention",
    )(heads(q), ckt, cvt, dt[:, :, None, :], heads(kb), heads(vb),
      jnp.swapaxes(cn[:, :, :H_F], 1, 2)[..., None], cnt[:, :, None, :])
    return jnp.swapaxes(o, 1, 2).reshape(B, T, D_F)


def _channel_tail(h, p_ref, gpre_ref, wg_ref, wu_ref, wd_ref, gpost_ref, gple_ref, wpg_ref, wpp_ref,
                  out_ref, act_ref):
    xn = _rms(h, gpre_ref[...]).astype(BF16)
    for c in range(0, D_FF, FFN_CHUNK):
        gate = jnp.dot(xn, wg_ref[:, c:c + FFN_CHUNK], preferred_element_type=F32)
        up = jnp.dot(xn, wu_ref[:, c:c + FFN_CHUNK], preferred_element_type=F32)
        act_ref[:, c:c + FFN_CHUNK] = (gate * jax.nn.sigmoid(gate) * up).astype(BF16)
    h = h + _rms(jnp.dot(act_ref[...], wd_ref[...], preferred_element_type=F32), gpost_ref[...])
    gate = jax.nn.sigmoid(_bdot(_rms(h, gple_ref[...]), wpg_ref[...]))
    out_ref[...] = h + gate * _bdot(p_ref[...], wpp_ref[...])


def _l0_tail_kernel(ya_ref, o_ref, x_ref, p_ref, wout_ref, gmix_ref, *tail):
    y = (jnp.dot(ya_ref[...], wout_ref[0:D_A, :], preferred_element_type=F32)
         + jnp.dot(o_ref[...], wout_ref[D_A:D_A + D_F, :], preferred_element_type=F32))
    _channel_tail(x_ref[...] + _rms(y, gmix_ref[...]), p_ref, *tail)


def _l1_tail_kernel(o_ref, r_ref, k_ref, v_ref, g_ref, h_ref, p_ref, lnw_ref, lnb_ref, rk_ref, bd_ref,
                    wo_ref, gmix_ref, *tail):
    bd = bd_ref[...]
    o = o_ref[...]
    d = o - _head_allsum(o, bd, exact=False) * (1.0 / HD_R)
    var = _head_allsum(d * d, bd, exact=False) * (1.0 / HD_R)
    on = d * lax.rsqrt(var + GN_EPS) * lnw_ref[...] + lnb_ref[...]
    rk = r_ref[...].astype(F32) * k_ref[...].astype(F32) * rk_ref[...]
    bonus = _head_allsum(rk, bd, exact=False) * v_ref[...].astype(F32)
    y = _bdot((on + bonus) * g_ref[...].astype(F32), wo_ref[...])
    _channel_tail(h_ref[...] + _rms(y, gmix_ref[...]), p_ref, *tail)


def _tail_call(kernel, layer, tokens, consts, name):
    n = tokens[0].shape[0]
    tm = _tile(n, 512)

    def tok(a):
        if a.ndim == 2:
            return pl.BlockSpec((tm, a.shape[1]), lambda i: (i, 0))
        return pl.BlockSpec((None, tm, a.shape[2]), lambda i: (layer, i, 0))

    def res(a):
        if a.ndim == 2:
            return pl.BlockSpec(a.shape, lambda i: (0, 0), pipeline_mode=pl.Buffered(1))
        return pl.BlockSpec((None,) + a.shape[1:], lambda i: (layer, 0, 0), pipeline_mode=pl.Buffered(1))

    return pl.pallas_call(
        kernel, grid=(n // tm,),
        in_specs=[tok(t) for t in tokens] + [res(c) for c in consts],
        out_specs=pl.BlockSpec((tm, D_MODEL), lambda i: (i, 0)),
        out_shape=jax.ShapeDtypeStruct((n, D_MODEL), F32),
        scratch_shapes=[pltpu.VMEM((tm, D_FF), BF16)],
        compiler_params=_cparams("parallel"), name=name,
    )(*tokens, *consts)


def _rwkv_prep_kernel(h_ref, sprev_ref, gpre_ref, mu_ref, wr_ref, wk_ref, wv_ref, w0_ref, w1_ref, w2_ref,
                      a0_ref, a1_ref, a2_ref, g1_ref, g2_ref, kk_ref, ka_ref, bd_ref,
                      r_out, lw_out, k_out, v_out, kk_out, b_out, g_out, shift_out, carry_ref):
    ts = h_ref.shape[0]

    @pl.when(pl.program_id(1) == 0)
    def _():
        carry_ref[7:8, :] = sprev_ref[...]

    xn = _rms(h_ref[...], gpre_ref[...])
    row = lax.broadcasted_iota(jnp.int32, xn.shape, 0)
    xx = jnp.where(row == 0, carry_ref[7:8, :], pltpu.roll(xn, 1, 0)) - xn
    carry_ref[...] = xn[ts - 8:ts, :]
    shift_out[...] = xn[ts - 1:ts, :]
    xnb, xxb, mub = xn.astype(BF16), xx.astype(BF16), mu_ref[...].astype(BF16)
    mix = lambda n: xnb + xxb * mub[n:n + 1, :]
    r = _bdot(mix(0), wr_ref[...])
    k = _bdot(mix(2), wk_ref[...])
    v = _bdot(mix(3), wv_ref[...])
    wl = w0_ref[...] + _bdot(jnp.tanh(_bdot(mix(1), w1_ref[...])), w2_ref[...])
    a = jax.nn.sigmoid(a0_ref[...] + _bdot(_bdot(mix(4), a1_ref[...]), a2_ref[...]))
    g = _bdot(jax.nn.sigmoid(_bdot(mix(5), g1_ref[...])), g2_ref[...])
    kkr = k * kk_ref[...]
    kk = kkr * lax.rsqrt(jnp.maximum(_head_allsum(kkr * kkr, bd_ref[...]), L2_EPS * L2_EPS))
    r_out[...] = r.astype(BF16)
    lw_out[...] = -DECAY_SCALE * jax.nn.sigmoid(wl)
    k_out[...] = (k * (1.0 + (a - 1.0) * ka_ref[...])).astype(BF16)
    v_out[...] = v.astype(BF16)
    kk_out[...] = kk.astype(BF16)
    b_out[...] = (kk * a).astype(BF16)
    g_out[...] = g.astype(BF16)


def _rwkv_prep(h, shift_prev, consts):
    B, S, D = h.shape
    ts = _tile(S, 512)
    tok = pl.BlockSpec((None, ts, D), lambda b, s: (b, s, 0))
    row = pl.BlockSpec((None, 1, D), lambda b, s: (b, 0, 0))
    full = lambda a: pl.BlockSpec(a.shape, lambda b, s: (0,) * a.ndim)
    big = lambda dt: jax.ShapeDtypeStruct((B, S, D), dt)
    return pl.pallas_call(
        _rwkv_prep_kernel, grid=(B, S // ts),
        in_specs=[tok, row] + [full(c) for c in consts],
        out_specs=(tok,) * 7 + (row,),
        out_shape=(big(BF16), big(F32)) + (big(BF16),) * 5 + (jax.ShapeDtypeStruct((B, 1, D), F32),),
        scratch_shapes=[pltpu.VMEM((8, D), F32)],
        compiler_params=_cparams("parallel", "arbitrary"), name="rwkv_prep",
    )(h, shift_prev, *consts)


def _wkv_kernel(r_ref, lw_ref, k_ref, v_ref, kk_ref, b_ref, s0_ref, o_ref, s_ref, sbd_ref):
    c = pl.program_id(1)
    nb, L = r_ref.shape[0], r_ref.shape[1]
    W = 2 * HD_R
    assert L == HD_R
    n_pairs = H_R // 2
    mm = functools.partial(jnp.dot, preferred_element_type=F32)
    units = [(bi, p) for bi in range(nb) for p in range(n_pairs)]

    @pl.when(c == 0)
    def _():
        zero = jnp.zeros((HD_R, HD_R), F32)
        for u, (bi, p) in enumerate(units):
            sbd_ref[u] = jnp.concatenate([jnp.concatenate([s0_ref[bi, 2 * p], zero], axis=1),
                                          jnp.concatenate([zero, s0_ref[bi, 2 * p + 1]], axis=1)], axis=0)

    at, rt, bt, kt, w_end = [], [], [], [], []
    for bi in range(nb):
        lw = lw_ref[bi]
        row = lax.broadcasted_iota(jnp.int32, lw.shape, 0)
        cum = lw
        sh = 1
        while sh < L:
            cum = cum + jnp.where(row >= sh, pltpu.roll(cum, sh, 0), 0.0)
            sh *= 2
        w_inv = jnp.exp(-cum)
        at.append((-kk_ref[bi].astype(F32) * jnp.exp(cum - lw)).astype(BF16))
        rt.append((r_ref[bi].astype(F32) * jnp.exp(cum)).astype(BF16))
        bt.append((b_ref[bi].astype(F32) * w_inv).astype(BF16))
        kt.append((k_ref[bi].astype(F32) * w_inv).astype(BF16))
        w_end.append(jnp.exp(cum[L - 1:L, :]))

    ri = lax.broadcasted_iota(jnp.int32, (2 * L, W), 0)
    ci = lax.broadcasted_iota(jnp.int32, (2 * L, W), 1)
    causal = (ri % L + ri // L) > ci % L
    lo2 = ci < HD_R
    r1 = lax.broadcasted_iota(jnp.int32, (L, W), 0)
    c1 = lax.broadcasted_iota(jnp.int32, (L, W), 1)
    lo1 = c1 < HD_R
    same_head = (ri // HD_R) == (ci // HD_R)
    eye2 = (r1 == c1 % L).astype(F32)

    def blockdiag(x):
        return jnp.concatenate([jnp.where(lo1, x, jnp.zeros_like(x)), jnp.where(lo1, jnp.zeros_like(x), x)], axis=0)

    pairs = range(len(units))
    sl = [slice(p * W, (p + 1) * W) for _, p in units]
    ar = [jnp.concatenate([at[bi][:, sl[u]], rt[bi][:, sl[u]]], axis=0) for u, (bi, _) in enumerate(units)]
    bk = [jnp.concatenate([bt[bi][:, sl[u]], kt[bi][:, sl[u]]], axis=0) for u, (bi, _) in enumerate(units)]
    kb = [jnp.concatenate([kt[bi][:, sl[u]], bt[bi][:, sl[u]]], axis=0) for u, (bi, _) in enumerate(units)]
    v2 = [v_ref[bi, :, sl[u]] for u, (bi, _) in enumerate(units)]
    S2 = [sbd_ref[u] for u in pairs]
    zeros2 = jnp.zeros((2 * L, W), BF16)
    g_a = [jnp.where(causal, _dot_nt(jnp.where(lo2, ar[p], zeros2), bk[p]), 0.0) for p in pairs]
    g_b = [jnp.where(causal, _dot_nt(jnp.where(lo2, zeros2, ar[p]), kb[p]), 0.0) for p in pairs]
    gb2 = [jnp.where(lo2, g_a[p], g_b[p]) for p in pairs]
    gk2 = [jnp.where(lo2, g_b[p], g_a[p]).astype(BF16) for p in pairs]
    zv = jnp.zeros((L, W), BF16)
    vx = [jnp.concatenate([jnp.where(lo1, zv, v2[p]), jnp.where(lo1, v2[p], zv)], axis=0) for p in pairs]
    xs = [_dot_nt(ar[p], S2[p].astype(BF16)) + mm(gk2[p], vx[p]) for p in pairs]
    inv = [eye2 + g[:L] for g in gb2]
    pw = [g[:L].astype(BF16) for g in gb2]
    pw = [mm(q, blockdiag(q)) for q in pw]
    sq = 2
    while sq < L:
        pwb = [q.astype(BF16) for q in pw]
        both = [mm(jnp.concatenate([inv[p].astype(BF16), pwb[p]], axis=0), blockdiag(pwb[p])) for p in pairs]
        inv = [inv[p] + both[p][:L] for p in pairs]
        pw = [b_[L:] for b_ in both]
        sq *= 2
    ub = [mm(inv[p].astype(BF16), blockdiag(xs[p][:L].astype(BF16))).astype(BF16) for p in pairs]
    for u, (bi, _) in enumerate(units):
        o_ref[bi, :, sl[u]] = xs[u][L:] + mm(gb2[u][L:].astype(BF16), blockdiag(ub[u]))
    for u, (bi, _) in enumerate(units):
        upd = _dot_tn(jnp.concatenate([ub[u], v2[u]], axis=0), bk[u])
        sbd_ref[u] = (S2[u] + jnp.where(same_head, upd, 0.0)) * w_end[bi][:, sl[u]]

    @pl.when(c == pl.num_programs(1) - 1)
    def _():
        for u, (bi, p) in enumerate(units):
            s_ref[bi, 2 * p] = sbd_ref[u, 0:HD_R, 0:HD_R]
            s_ref[bi, 2 * p + 1] = sbd_ref[u, HD_R:W, HD_R:W]


def _wkv(r, lw, k, v, kk, b, s0):
    B, S, D = r.shape
    L = _tile(S, WKV_CHUNK)
    nb = _tile(B, WKV_SEQS)
    tok = pl.BlockSpec((nb, L, D), lambda b_, c: (b_, c, 0))
    st = pl.BlockSpec((nb, H_R, HD_R, HD_R), lambda b_, c: (b_, 0, 0, 0))
    return pl.pallas_call(
        _wkv_kernel, grid=(B // nb, S // L),
        in_specs=[tok] * 6 + [st], out_specs=(tok, st),
        out_shape=(jax.ShapeDtypeStruct((B, S, D), F32), jax.ShapeDtypeStruct(s0.shape, F32)),
        scratch_shapes=[pltpu.VMEM((nb * H_R // 2, 2 * HD_R, 2 * HD_R), F32)],
        compiler_params=_cparams("parallel", "arbitrary"), name="wkv_chunked",
    )(r, lw, k, v, kk, b, s0)


def _prep_weights(W):
    bf = lambda a: a.astype(BF16)
    row = lambda a: a.reshape(1, -1)
    P = dict(W)
    w_in = W["e_w_in"][0]
    n_main = 3 * D_A + 3 * D_F
    P["w_main"] = bf(w_in[:, :n_main])
    P["w_fl"] = bf(jnp.pad(w_in[:, n_main:], ((0, 0), (0, LANES - H_F))))
    P["b_f"] = jnp.pad(W["e_b_f"][0], (0, LANES - H_F)).reshape(1, LANES)
    P["w_out"] = bf(W["e_w_out"][0])
    for n in ("f_w_gate", "f_w_up", "f_w_down", "ple_gate", "ple_proj"):
        P[n] = bf(W[n])
    for n in ("r_w_r", "r_w_k", "r_w_v", "r_w_o", "r_w1", "r_w2", "r_a1", "r_a2", "r_g1", "r_g2"):
        P[n] = bf(W[n][0])
    for n in ("r_w0", "r_a0", "r_k_k", "r_k_a", "r_ln_w", "r_ln_b"):
        P[n] = row(W[n][0])
    P["r_r_k"] = W["r_r_k"][0].reshape(1, D_MODEL)
    P["r_mu"] = W["r_mu"][0]
    P["bd"] = _head_blockdiag(HD_R)
    return P


def _trunk(x, p, fox_cache, conv_prev, shift_prev, wkv_prev, P):
    B, S, D = x.shape
    n = B * S
    flat = lambda a: a.reshape(n, a.shape[-1])
    vec = lambda name, i: P[name][i].reshape(1, D)

    ya, q, k, v, kb, vb, vt, lf, cst = _inproj(x, vec("mix_norm_pre", 0), P["w_main"], P["w_fl"], P["b_f"],
                                              conv_prev, P["e_conv_w"][0])
    if fox_cache is None:
        _, qc, kc = _cumsum(lf, carriers=True)
        o = _fox_prompt(q, kb, vt, qc, kc)
    else:
        ck, cv, clf = fox_cache
        Pn = ck.shape[1]
        c = _cumsum(lf, carriers=False)
        ct = jnp.swapaxes(c[:, :, :8], 1, 2)
        dt = _cumsum_lanes(jnp.swapaxes(clf, 1, 2).reshape(B * H_F, Pn)).reshape(B, H_F, Pn)
        to_t = lambda a: jnp.transpose(a, (0, 2, 3, 1))
        o = _fox_sample(q, to_t(ck), to_t(cv), dt, kb, vb, c, ct)
    tail_consts = lambda i: [vec("ffn_norm_pre", i), P["f_w_gate"], P["f_w_up"], P["f_w_down"],
                             vec("ffn_norm_post", i), vec("ple_norm", i), P["ple_gate"], P["ple_proj"]]
    p_all = p.reshape(p.shape[0], n, p.shape[-1])
    h = _tail_call(_l0_tail_kernel, 0, [flat(ya), flat(o), flat(x), p_all],
                   [P["w_out"], vec("mix_norm_post", 0)] + tail_consts(0), "l0_outproj_ffn_ple")

    consts = [vec("mix_norm_pre", 1), P["r_mu"], P["r_w_r"], P["r_w_k"], P["r_w_v"], P["r_w0"], P["r_w1"], P["r_w2"],
              P["r_a0"], P["r_a1"], P["r_a2"], P["r_g1"], P["r_g2"], P["r_k_k"], P["r_k_a"], P["bd"]]
    r, lw, km, vv, kk, bb, g, shift = _rwkv_prep(h.reshape(B, S, D), shift_prev.reshape(B, 1, D), consts)
    o1, wkv = _wkv(r, lw, km, vv, kk, bb, wkv_prev)
    h = _tail_call(_l1_tail_kernel, 1, [flat(o1), flat(r), flat(km), flat(vv), flat(g), h, p_all],
                   [P["r_ln_w"], P["r_ln_b"], P["r_r_k"], P["bd"], P["r_w_o"], vec("mix_norm_post", 1)]
                   + tail_consts(1), "l1_rwkvout_ffn_ple")

    return (h.reshape(B, S, D), k.reshape(1, B, S, H_F, HD_F), v.reshape(1, B, S, H_F, HD_F),
            lf[None, :, :, :H_F], cst[None], shift.reshape(1, B, D), wkv[None])


def kernel(x_prompt, x_sample, p_prompt, p_sample, cache_k, cache_v, cache_logf, state_conv, state_shift, state_wkv, mix_norm_pre, mix_norm_post, ffn_norm_pre, ffn_norm_post, e_w_in, e_b_f, e_conv_w, e_w_out, r_mu, r_w_r, r_w_k, r_w_v, r_w_o, r_w0, r_w1, r_w2, r_a0, r_a1, r_a2, r_g1, r_g2, r_k_k, r_k_a, r_r_k, r_ln_w, r_ln_b, f_w_gate, f_w_up, f_w_down, ple_norm, ple_gate, ple_proj):
    W = dict(mix_norm_pre=mix_norm_pre, mix_norm_post=mix_norm_post, ffn_norm_pre=ffn_norm_pre,
             ffn_norm_post=ffn_norm_post, e_w_in=e_w_in, e_b_f=e_b_f, e_conv_w=e_conv_w, e_w_out=e_w_out,
             r_mu=r_mu, r_w_r=r_w_r, r_w_k=r_w_k, r_w_v=r_w_v, r_w_o=r_w_o, r_w0=r_w0, r_w1=r_w1, r_w2=r_w2,
             r_a0=r_a0, r_a1=r_a1, r_a2=r_a2, r_g1=r_g1, r_g2=r_g2, r_k_k=r_k_k, r_k_a=r_k_a, r_r_k=r_r_k,
             r_ln_w=r_ln_w, r_ln_b=r_ln_b, f_w_gate=f_w_gate, f_w_up=f_w_up, f_w_down=f_w_down,
             ple_norm=ple_norm, ple_gate=ple_gate, ple_proj=ple_proj)
    P = _prep_weights(W)
    bp = x_prompt.shape[0]
    y_p, k_p, v_p, lf_p, c_p, sh_p, s_p = _trunk(
        x_prompt, p_prompt, None, jnp.zeros((bp, 2, D_A), F32), jnp.zeros((bp, D_MODEL), F32),
        jnp.zeros((bp, H_R, HD_R, HD_R), F32), P)
    y_s, k_s, v_s, lf_s, c_s, sh_s, s_s = _trunk(
        x_sample, p_sample, (cache_k[0], cache_v[0], cache_logf[0]), state_conv[0], state_shift[0], state_wkv[0], P)
    return (y_p, y_s, k_p, v_p, lf_p, c_p, sh_p, s_p, k_s, v_s, lf_s, c_s, sh_s, s_s)
```

```python
import functools

import jax
import jax.numpy as jnp
import numpy as np
from jax import lax
from jax.experimental import pallas as pl
from jax.experimental.pallas import tpu as pltpu

D_MODEL = 1024
D_A = 512
H_F = 8
HD_F = 64
D_F = H_F * HD_F
HD_R = 64
H_R = D_MODEL // HD_R
PLE_DIM = 256
D_FF = 2816
NORM_EPS = 1e-6
GN_EPS = 64e-5
L2_EPS = 1e-12
NEG_INF = -1e30

LOG2E = 1.4426950408889634
DECAY_SCALE = 0.6065306597126334
LANES = 128
FFN_CHUNK = 256
SAMPLE_HEADS = 2
ATTN_TILE = 1024
ATTN_Q_SUB = 256
WKV_SEQS = 2
WKV_CHUNK = 64
VMEM_LIMIT = 48 * 1024 * 1024

BF16 = jnp.bfloat16
F32 = jnp.float32


def _cparams(*sem):
    return pltpu.CompilerParams(dimension_semantics=sem, vmem_limit_bytes=VMEM_LIMIT)


def _tile(n, pref):
    t = min(n, pref)
    assert n % t == 0, (n, pref)
    return t


def _rms(x, g):
    return x * lax.rsqrt(jnp.mean(x * x, axis=-1, keepdims=True) + NORM_EPS) * g


def _bdot(a, w):
    return jnp.dot(a.astype(BF16), w, preferred_element_type=F32)


def _dot_nt(a, b, **kw):
    return lax.dot_general(a, b, (((1,), (1,)), ((), ())), preferred_element_type=F32, **kw)


def _dot_tn(a, b, **kw):
    return lax.dot_general(a, b, (((0,), (0,)), ((), ())), preferred_element_type=F32, **kw)


def _softplus(y):
    return jnp.maximum(y, 0.0) + jnp.log1p(jnp.exp(-jnp.abs(y)))


def _split3(x):
    hi = x.astype(BF16)
    r1 = x - hi.astype(F32)
    mid = r1.astype(BF16)
    lo = (r1 - mid.astype(F32)).astype(BF16)
    return hi, mid, lo


def _head_allsum(x, bd, exact=True):
    hi = x.astype(BF16)
    w = bd.shape[0]
    slabs = [jnp.dot(hi[:, t:t + w], bd, preferred_element_type=F32) for t in range(0, x.shape[1], w)]
    if exact:
        lo = (x - hi.astype(F32)).astype(BF16)
        slabs = [sb + jnp.dot(lo[:, t:t + w], bd, preferred_element_type=F32)
                 for sb, t in zip(slabs, range(0, x.shape[1], w))]
    return jnp.concatenate(slabs, axis=1)


def _head_blockdiag(hd, width=256):
    idx = jnp.arange(width) // hd
    return (idx[:, None] == idx[None, :]).astype(BF16)


def _prev_rows(first_ref, carry_ref, rows, nb, ts):
    if nb == 1:
        return carry_ref[rows]
    w = first_ref.shape[-1]
    return jnp.broadcast_to(first_ref[:, rows, :], (nb, ts, w)).reshape(nb * ts, w)


def _inproj_kernel(x_ref, g_ref, w_ref, wfl_ref, bf_ref, cprev_ref, cw_ref,
                   ya_ref, q_ref, k_ref, v_ref, kb_ref, vx_ref, lf_ref, cst_ref, carry_ref, *, v_transposed):
    nb, ts, D = x_ref.shape
    n = nb * ts
    if nb == 1:
        @pl.when(pl.program_id(1) == 0)
        def _():
            carry_ref[6:8, :] = cprev_ref[0]
        p0, p1 = carry_ref[6:7, :], carry_ref[7:8, :]
    else:
        p0 = _prev_rows(cprev_ref, None, slice(0, 1), nb, ts)
        p1 = _prev_rows(cprev_ref, None, slice(1, 2), nb, ts)

    xn = _rms(x_ref[...].reshape(n, D), g_ref[...]).astype(BF16)
    z = [jnp.dot(xn, w_ref[:, c * D_A:(c + 1) * D_A], preferred_element_type=F32) for c in range(6)]
    ax, a_b, a_c, q, k, v = z
    fl = jnp.dot(xn, wfl_ref[...], preferred_element_type=F32) + bf_ref[...]
    lf_ref[...] = (-_softplus(-fl)).reshape(nb, ts, LANES)
    u = a_c * ax
    t = lax.broadcasted_iota(jnp.int32, u.shape, 0) % ts
    um1 = jnp.where(t == 0, p1, pltpu.roll(u, 1, 0))
    um2 = jnp.where(t == 0, p0, jnp.where(t == 1, p1, pltpu.roll(u, 2, 0)))
    cu = cw_ref[0:1, :] * um2 + cw_ref[1:2, :] * um1 + cw_ref[2:3, :] * u
    ya_ref[...] = (a_b * cu).astype(BF16).reshape(nb, ts, D_A)
    u3 = u.reshape(nb, ts, D_A)
    if nb == 1:
        carry_ref[...] = u[ts - 8:ts, :]
    cst_ref[...] = u3[:, ts - 2:ts, :]
    shape = (nb, ts, D_F)
    q_ref[...] = (q * (HD_F ** -0.5 * LOG2E)).astype(BF16).reshape(shape)
    k_ref[...] = k.reshape(shape)
    v_ref[...] = v.reshape(shape)
    kb_ref[...] = k.astype(BF16).reshape(shape)
    if v_transposed:
        vx_ref[0] = v.T.astype(BF16)
    else:
        vx_ref[...] = v.astype(BF16).reshape(shape)


def _inproj(x, g, w_main, w_fl, b_f, conv_prev, conv_w, v_transposed):
    B, S, D = x.shape
    ts = _tile(S, 512)
    nb = _tile(B, 512 // ts) if ts == S and not v_transposed else 1
    tok = lambda w: pl.BlockSpec((nb, ts, w), lambda b, s: (b, s, 0))
    full = lambda a: pl.BlockSpec(a.shape, lambda b, s: (0,) * a.ndim)
    st = pl.BlockSpec((nb, 2, D_A), lambda b, s: (b, 0, 0))
    out_shape = (
        jax.ShapeDtypeStruct((B, S, D_A), BF16),
        jax.ShapeDtypeStruct((B, S, D_F), BF16),
        jax.ShapeDtypeStruct((B, S, D_F), F32),
        jax.ShapeDtypeStruct((B, S, D_F), F32),
        jax.ShapeDtypeStruct((B, S, D_F), BF16),
        jax.ShapeDtypeStruct((B, D_F, S) if v_transposed else (B, S, D_F), BF16),
        jax.ShapeDtypeStruct((B, S, LANES), F32),
        jax.ShapeDtypeStruct((B, 2, D_A), F32),
    )
    vx_spec = pl.BlockSpec((nb, D_F, ts), lambda b, s: (b, 0, s)) if v_transposed else tok(D_F)
    return pl.pallas_call(
        functools.partial(_inproj_kernel, v_transposed=v_transposed),
        grid=(B // nb, S // ts),
        in_specs=[tok(D), full(g), full(w_main), full(w_fl), full(b_f), st, full(conv_w)],
        out_specs=(tok(D_A), tok(D_F), tok(D_F), tok(D_F), tok(D_F), vx_spec, tok(LANES), st),
        out_shape=out_shape,
        scratch_shapes=[pltpu.VMEM((8, D_A), F32)],
        compiler_params=_cparams("parallel", "arbitrary"),
        name="l0_inproj_conv",
    )(x, g, w_main, w_fl, b_f, conv_prev, conv_w)


def _scan_rows(x):
    n = x.shape[0]
    row = lax.broadcasted_iota(jnp.int32, x.shape, 0)
    sh = 1
    while sh < n:
        x = x + jnp.where(row >= sh, pltpu.roll(x, sh, 0), 0.0)
        sh *= 2
    return x


def _cumsum_kernel(x_ref, o_ref):
    o_ref[...] = _scan_rows(x_ref[...]) * LOG2E


def _cumsum_lanes_kernel(x_ref, o_ref):
    x = x_ref[...]
    n = x.shape[1]
    col = lax.broadcasted_iota(jnp.int32, x.shape, 1)
    sh = 1
    while sh < n:
        x = x + jnp.where(col >= sh, pltpu.roll(x, sh, 1), 0.0)
        sh *= 2
    o_ref[...] = (x - x[:, n - 1:n]) * LOG2E


def _cumsum_lanes(x):
    spec = pl.BlockSpec(x.shape, lambda i: (0, 0))
    return pl.pallas_call(
        _cumsum_lanes_kernel, grid=(1,), in_specs=[spec], out_specs=spec,
        out_shape=jax.ShapeDtypeStruct(x.shape, F32),
        compiler_params=_cparams("arbitrary"), name="cache_logf_cumsum",
    )(x)


def _cumsum_carrier_kernel(x_ref, pq_ref, pk_ref, oq_ref, ok_ref, o_ref, qc_ref, kc_ref):
    c = _scan_rows(x_ref[...]) * LOG2E
    o_ref[...] = c
    parts = jnp.concatenate(_split3(c), axis=1)
    qc_ref[...] = (jnp.dot(parts, pq_ref[...], preferred_element_type=F32) + oq_ref[...]).astype(BF16)
    kc_ref[...] = (jnp.dot(parts, pk_ref[...], preferred_element_type=F32) + ok_ref[...]).astype(BF16)


def _carrier_lane(h, slot):
    return LANES * (h // 2) + (HD_F if h % 2 == 0 else 0) + slot


def _carrier_placement():
    pq = np.zeros((3 * LANES, D_F), np.float32)
    pk = np.zeros((3 * LANES, D_F), np.float32)
    oq = np.zeros((1, D_F), np.float32)
    ok = np.zeros((1, D_F), np.float32)
    for h in range(H_F):
        for part in range(3):
            pq[part * LANES + h, _carrier_lane(h, part)] = 1.0
            ok[0, _carrier_lane(h, part)] = 1.0
            pk[part * LANES + h, _carrier_lane(h, 3 + part)] = -1.0
            oq[0, _carrier_lane(h, 3 + part)] = 1.0
    return jnp.asarray(pq, BF16), jnp.asarray(pk, BF16), jnp.asarray(oq), jnp.asarray(ok)


def _cumsum(x, carriers):
    B, S, W = x.shape
    spec = pl.BlockSpec((None, S, W), lambda b: (b, 0, 0))
    if not carriers:
        return pl.pallas_call(
            _cumsum_kernel, grid=(B,), in_specs=[spec], out_specs=spec,
            out_shape=jax.ShapeDtypeStruct(x.shape, F32),
            compiler_params=_cparams("parallel"), name="logf_cumsum",
        )(x)
    consts = _carrier_placement()
    cspec = pl.BlockSpec((None, S, D_F), lambda b: (b, 0, 0))
    return pl.pallas_call(
        _cumsum_carrier_kernel, grid=(B,),
        in_specs=[spec] + [pl.BlockSpec(a.shape, lambda b: (0, 0)) for a in consts],
        out_specs=(spec, cspec, cspec),
        out_shape=(jax.ShapeDtypeStruct(x.shape, F32), jax.ShapeDtypeStruct((B, S, D_F), BF16),
                   jax.ShapeDtypeStruct((B, S, D_F), BF16)),
        compiler_params=_cparams("parallel"), name="logf_cumsum_carriers",
    )(x, *consts)


def _fox_prompt_kernel(q_ref, k_ref, vt_ref, qc_ref, kc_ref, o_ref, m_ref, acc_ref):
    i = pl.program_id(1)
    j = pl.program_id(2)
    tq, tk = q_ref.shape[0], k_ref.shape[0]
    tqs = min(tq, ATTN_Q_SUB)

    @pl.when(j == 0)
    def _():
        m_ref[...] = jnp.full(m_ref.shape, NEG_INF, F32)
        acc_ref[...] = jnp.zeros(acc_ref.shape, F32)

    def update(diag):
        lane_q = lax.broadcasted_iota(jnp.int32, (tq, LANES), 1)
        lane_k = lax.broadcasted_iota(jnp.int32, (tk, LANES), 1)
        row_v = lax.broadcasted_iota(jnp.int32, (LANES, tk), 0)
        for pair in range(H_F // 2):
            sl = slice(pair * LANES, (pair + 1) * LANES)
            q2, qc2, k2, kc2, vt2 = q_ref[:, sl], qc_ref[:, sl], k_ref[:, sl], kc_ref[:, sl], vt_ref[sl, :]
            own = lambda idx, e: (idx < HD_F) == (e == 0)
            qa = [jnp.where(own(lane_q, e), q2, qc2) for e in range(2)]
            ka = [jnp.where(own(lane_k, e), k2, kc2) for e in range(2)]
            va = [jnp.where(own(row_v, e), vt2, jnp.ones_like(vt2)) for e in range(2)]
            units = [(e, qs) for e in range(2) for qs in range(tq // tqs)]
            cs = [slice(qs * tqs, (qs + 1) * tqs) for _, qs in units]
            nk = [(qs + 1) * tqs if diag else tk for _, qs in units]
            st = [_dot_nt(ka[e][:nk[u]], qa[e][cs[u]]) for u, (e, _) in enumerate(units)]
            if diag:
                for u, (_, qs) in enumerate(units):
                    key = lax.broadcasted_iota(jnp.int32, (nk[u], tqs), 0)
                    qry = lax.broadcasted_iota(jnp.int32, (nk[u], tqs), 1) + qs * tqs
                    st[u] = jnp.where(key <= qry, st[u], NEG_INF)
            m_old = [m_ref[2 * pair + e, :, cs[u]] for u, (e, _) in enumerate(units)]
            m_new = [jnp.maximum(m_old[u], jnp.max(st[u], axis=0, keepdims=True)) for u in range(len(units))]
            pt = [jnp.exp2(st[u] - m_new[u]).astype(BF16) for u in range(len(units))]
            pv = [jnp.dot(va[e][:, :nk[u]], pt[u], preferred_element_type=F32) for u, (e, _) in enumerate(units)]
            for u, (e, _) in enumerate(units):
                h = 2 * pair + e
                acc_ref[h, :, cs[u]] = jnp.exp2(m_old[u] - m_new[u]) * acc_ref[h, :, cs[u]] + pv[u]
                m_ref[h, :, cs[u]] = m_new[u]

    @pl.when(j < i)
    def _():
        update(False)

    @pl.when(j == i)
    def _():
        update(True)
        for pair in range(H_F // 2):
            a, b = acc_ref[2 * pair], acc_ref[2 * pair + 1]
            ot = jnp.concatenate([a[:HD_F] / a[HD_F:HD_F + 1], b[HD_F:] / b[0:1]], axis=0)
            o_ref[:, pair * LANES:(pair + 1) * LANES] = ot.T.astype(o_ref.dtype)


def _fox_prompt(q, kb, vt, qc, kc):
    B, S, _ = q.shape
    t = _tile(S, ATTN_TILE)
    n = S // t
    qs = pl.BlockSpec((None, t, D_F), lambda b, i, j: (b, i, 0))
    ks = pl.BlockSpec((None, t, D_F), lambda b, i, j: (b, jnp.minimum(j, i), 0))
    vs = pl.BlockSpec((None, D_F, t), lambda b, i, j: (b, 0, jnp.minimum(j, i)))
    return pl.pallas_call(
        _fox_prompt_kernel, grid=(B, n, n),
        in_specs=[qs, ks, vs, qs, ks], out_specs=qs,
        out_shape=jax.ShapeDtypeStruct((B, S, D_F), BF16),
        scratch_shapes=[pltpu.VMEM((H_F, 1, t), F32), pltpu.VMEM((H_F, LANES, t), F32)],
        compiler_params=_cparams("parallel", "parallel", "arbitrary"),
        name="fox_prompt_attention",
    )(q, kb, vt, qc, kc)


def _fox_sample_kernel(q_ref, ckt_ref, cvt_ref, dt_ref, kn_ref, vn_ref, cn_ref, cnt_ref, o_ref,
                       m_ref, l_ref, acc_ref):
    j = pl.program_id(2)
    nh, T = q_ref.shape[0], q_ref.shape[1]
    heads = range(nh)

    @pl.when(j == 0)
    def _():
        m_ref[...] = jnp.full(m_ref.shape, NEG_INF, F32)
        l_ref[...] = jnp.zeros(l_ref.shape, F32)
        acc_ref[...] = jnp.zeros(acc_ref.shape, F32)

    def update(s, pv):
        m_old = [m_ref[h] for h in heads]
        m_new = [jnp.maximum(m_old[h], jnp.max(s[h], axis=-1, keepdims=True)) for h in heads]
        p = [jnp.exp2(s[h] - m_new[h]) for h in heads]
        o = [pv(h, p[h].astype(BF16)) for h in heads]
        for h in heads:
            alpha = jnp.exp2(m_old[h] - m_new[h])
            l_ref[h] = alpha * l_ref[h] + jnp.sum(p[h], axis=-1, keepdims=True)
            acc_ref[h] = alpha * acc_ref[h] + o[h]
            m_ref[h] = m_new[h]

    q = [q_ref[h] for h in heads]
    cn = [cn_ref[h] for h in heads]
    kt = [ckt_ref[h].astype(BF16) for h in heads]
    vt = [cvt_ref[h].astype(BF16) for h in heads]
    update([jnp.dot(q[h], kt[h], preferred_element_type=F32) + (cn[h] - dt_ref[h]) for h in heads],
           lambda h, p: _dot_nt(p, vt[h]))

    @pl.when(j == pl.num_programs(2) - 1)
    def _():
        causal = lax.broadcasted_iota(jnp.int32, (T, T), 0) >= lax.broadcasted_iota(jnp.int32, (T, T), 1)
        update([jnp.where(causal, _dot_nt(q[h], kn_ref[h]) + (cn[h] - cnt_ref[h]), NEG_INF) for h in heads],
               lambda h, p: jnp.dot(p, vn_ref[h], preferred_element_type=F32))
        for h in heads:
            o_ref[h] = (acc_ref[h] / l_ref[h]).astype(o_ref.dtype)


def _fox_sample(q, ckt, cvt, dt, kb, vb, cn, cnt):
    B, T, _ = q.shape
    P = ckt.shape[-1]
    tk = _tile(P, 4096)
    nh = SAMPLE_HEADS
    heads = lambda a: jnp.swapaxes(a.reshape(B, T, H_F, HD_F), 1, 2)
    new = lambda w: pl.BlockSpec((None, nh, T, w), lambda b, h, j: (b, h, 0, 0))
    cache = pl.BlockSpec((None, nh, HD_F, tk), lambda b, h, j: (b, h, 0, j))
    o = pl.pallas_call(
        _fox_sample_kernel, grid=(B, H_F // nh, P // tk),
        in_specs=[new(HD_F), cache, cache, pl.BlockSpec((None, nh, 1, tk), lambda b, h, j: (b, h, 0, j)),
                  new(HD_F), new(HD_F), new(1), pl.BlockSpec((None, nh, 1, T), lambda b, h, j: (b, h, 0, 0))],
        out_specs=new(HD_F),
        out_shape=jax.ShapeDtypeStruct((B, H_F, T, HD_F), BF16),
        scratch_shapes=[pltpu.VMEM((nh, T, 1), F32), pltpu.VMEM((nh, T, 1), F32), pltpu.VMEM((nh, T, HD_F), F32)],
        compiler_params=_cparams("parallel", "parallel", "arbitrary"),
        name="fox_sample_attention",
    )(heads(q), ckt, cvt, dt[:, :, None, :], heads(kb), heads(vb),
      jnp.swapaxes(cn[:, :, :H_F], 1, 2)[..., None], cnt[:, :, None, :])
    return jnp.swapaxes(o, 1, 2).reshape(B, T, D_F)


def _channel_tail(h, p_ref, gpre_ref, wg_ref, wu_ref, wd_ref, gpost_ref, gple_ref, wpg_ref, wpp_ref,
                  out_ref, act_ref):
    xn = _rms(h, gpre_ref[...]).astype(BF16)
    for c in range(0, D_FF, FFN_CHUNK):
        gate = jnp.dot(xn, wg_ref[:, c:c + FFN_CHUNK], preferred_element_type=F32)
        up = jnp.dot(xn, wu_ref[:, c:c + FFN_CHUNK], preferred_element_type=F32)
        act_ref[:, c:c + FFN_CHUNK] = (gate * jax.nn.sigmoid(gate) * up).astype(BF16)
    h = h + _rms(jnp.dot(act_ref[...], wd_ref[...], preferred_element_type=F32), gpost_ref[...])
    gate = jax.nn.sigmoid(_bdot(_rms(h, gple_ref[...]), wpg_ref[...]))
    out_ref[...] = h + gate * _bdot(p_ref[...], wpp_ref[...])


def _l0_tail_kernel(ya_ref, o_ref, x_ref, p_ref, wout_ref, gmix_ref, *tail):
    y = (jnp.dot(ya_ref[...], wout_ref[0:D_A, :], preferred_element_type=F32)
         + jnp.dot(o_ref[...], wout_ref[D_A:D_A + D_F, :], preferred_element_type=F32))
    _channel_tail(x_ref[...] + _rms(y, gmix_ref[...]), p_ref, *tail)


def _l1_tail_kernel(o_ref, r_ref, k_ref, v_ref, g_ref, h_ref, p_ref, lnw_ref, lnb_ref, rk_ref, bd_ref,
                    wo_ref, gmix_ref, *tail):
    bd = bd_ref[...]
    o = o_ref[...]
    d = o - _head_allsum(o, bd, exact=False) * (1.0 / HD_R)
    var = _head_allsum(d * d, bd, exact=False) * (1.0 / HD_R)
    on = d * lax.rsqrt(var + GN_EPS) * lnw_ref[...] + lnb_ref[...]
    rk = r_ref[...].astype(F32) * k_ref[...].astype(F32) * rk_ref[...]
    bonus = _head_allsum(rk, bd, exact=False) * v_ref[...].astype(F32)
    y = _bdot((on + bonus) * g_ref[...].astype(F32), wo_ref[...])
    _channel_tail(h_ref[...] + _rms(y, gmix_ref[...]), p_ref, *tail)


def _tail_call(kernel, layer, tokens, consts, name):
    n = tokens[0].shape[0]
    tm = _tile(n, 512)

    def tok(a):
        if a.ndim == 2:
            return pl.BlockSpec((tm, a.shape[1]), lambda i: (i, 0))
        return pl.BlockSpec((None, tm, a.shape[2]), lambda i: (layer, i, 0))

    def res(a):
        if a.ndim == 2:
            return pl.BlockSpec(a.shape, lambda i: (0, 0), pipeline_mode=pl.Buffered(1))
        return pl.BlockSpec((None,) + a.shape[1:], lambda i: (layer, 0, 0), pipeline_mode=pl.Buffered(1))

    return pl.pallas_call(
        kernel, grid=(n // tm,),
        in_specs=[tok(t) for t in tokens] + [res(c) for c in consts],
        out_specs=pl.BlockSpec((tm, D_MODEL), lambda i: (i, 0)),
        out_shape=jax.ShapeDtypeStruct((n, D_MODEL), F32),
        scratch_shapes=[pltpu.VMEM((tm, D_FF), BF16)],
        compiler_params=_cparams("parallel"), name=name,
    )(*tokens, *consts)


def _rwkv_prep_kernel(h_ref, sprev_ref, gpre_ref, mu_ref, wr_ref, wk_ref, wv_ref, w0_ref, w1_ref, w2_ref,
                      a0_ref, a1_ref, a2_ref, g1_ref, g2_ref, kk_ref, ka_ref, bd_ref,
                      r_out, lw_out, k_out, v_out, kk_out, b_out, g_out, shift_out, carry_ref):
    nb, ts, D = h_ref.shape
    n = nb * ts
    if nb == 1:
        @pl.when(pl.program_id(1) == 0)
        def _():
            carry_ref[7:8, :] = sprev_ref[0]
    prev = _prev_rows(sprev_ref, carry_ref, slice(7, 8) if nb == 1 else slice(0, 1), nb, ts)

    xn = _rms(h_ref[...].reshape(n, D), gpre_ref[...])
    t = lax.broadcasted_iota(jnp.int32, xn.shape, 0) % ts
    xx = jnp.where(t == 0, prev, pltpu.roll(xn, 1, 0)) - xn
    if nb == 1:
        carry_ref[...] = xn[ts - 8:ts, :]
    shift_out[...] = xn.reshape(nb, ts, D)[:, ts - 1:ts, :]
    xnb, xxb, mub = xn.astype(BF16), xx.astype(BF16), mu_ref[...].astype(BF16)
    mix = lambda n: xnb + xxb * mub[n:n + 1, :]
    r = _bdot(mix(0), wr_ref[...])
    k = _bdot(mix(2), wk_ref[...])
    v = _bdot(mix(3), wv_ref[...])
    wl = w0_ref[...] + _bdot(jnp.tanh(_bdot(mix(1), w1_ref[...])), w2_ref[...])
    a = jax.nn.sigmoid(a0_ref[...] + _bdot(_bdot(mix(4), a1_ref[...]), a2_ref[...]))
    g = _bdot(jax.nn.sigmoid(_bdot(mix(5), g1_ref[...])), g2_ref[...])
    kkr = k * kk_ref[...]
    kk = kkr * lax.rsqrt(jnp.maximum(_head_allsum(kkr * kkr, bd_ref[...]), L2_EPS * L2_EPS))
    shape = (nb, ts, D)
    r_out[...] = r.astype(BF16).reshape(shape)
    lw_out[...] = (-DECAY_SCALE * jax.nn.sigmoid(wl)).reshape(shape)
    k_out[...] = (k * (1.0 + (a - 1.0) * ka_ref[...])).astype(BF16).reshape(shape)
    v_out[...] = v.astype(BF16).reshape(shape)
    kk_out[...] = kk.astype(BF16).reshape(shape)
    b_out[...] = (kk * a).astype(BF16).reshape(shape)
    g_out[...] = g.astype(BF16).reshape(shape)


def _rwkv_prep(h, shift_prev, consts):
    B, S, D = h.shape
    ts = _tile(S, 512)
    nb = _tile(B, 512 // ts) if ts == S else 1
    tok = pl.BlockSpec((nb, ts, D), lambda b, s: (b, s, 0))
    row = pl.BlockSpec((nb, 1, D), lambda b, s: (b, 0, 0))
    full = lambda a: pl.BlockSpec(a.shape, lambda b, s: (0,) * a.ndim)
    big = lambda dt: jax.ShapeDtypeStruct((B, S, D), dt)
    return pl.pallas_call(
        _rwkv_prep_kernel, grid=(B // nb, S // ts),
        in_specs=[tok, row] + [full(c) for c in consts],
        out_specs=(tok,) * 7 + (row,),
        out_shape=(big(BF16), big(F32)) + (big(BF16),) * 5 + (jax.ShapeDtypeStruct((B, 1, D), F32),),
        scratch_shapes=[pltpu.VMEM((8, D), F32)],
        compiler_params=_cparams("parallel", "arbitrary"), name="rwkv_prep",
    )(h, shift_prev, *consts)


def _wkv_kernel(r_ref, lw_ref, k_ref, v_ref, kk_ref, b_ref, s0_ref, o_ref, s_ref, sbd_ref):
    c = pl.program_id(1)
    nb, L = r_ref.shape[0], r_ref.shape[1]
    W = 2 * HD_R
    assert L == HD_R
    n_pairs = H_R // 2
    mm = functools.partial(jnp.dot, preferred_element_type=F32)
    units = [(bi, p) for bi in range(nb) for p in range(n_pairs)]

    @pl.when(c == 0)
    def _():
        zero = jnp.zeros((HD_R, HD_R), F32)
        for u, (bi, p) in enumerate(units):
            sbd_ref[u] = jnp.concatenate([jnp.concatenate([s0_ref[bi, 2 * p], zero], axis=1),
                                          jnp.concatenate([zero, s0_ref[bi, 2 * p + 1]], axis=1)], axis=0)

    at, rt, bt, kt, w_end = [], [], [], [], []
    for bi in range(nb):
        lw = lw_ref[bi]
        row = lax.broadcasted_iota(jnp.int32, lw.shape, 0)
        cum = lw
        sh = 1
        while sh < L:
            cum = cum + jnp.where(row >= sh, pltpu.roll(cum, sh, 0), 0.0)
            sh *= 2
        w_inv = jnp.exp(-cum)
        at.append((-kk_ref[bi].astype(F32) * jnp.exp(cum - lw)).astype(BF16))
        rt.append((r_ref[bi].astype(F32) * jnp.exp(cum)).astype(BF16))
        bt.append((b_ref[bi].astype(F32) * w_inv).astype(BF16))
        kt.append((k_ref[bi].astype(F32) * w_inv).astype(BF16))
        w_end.append(jnp.exp(cum[L - 1:L, :]))

    ri = lax.broadcasted_iota(jnp.int32, (2 * L, W), 0)
    ci = lax.broadcasted_iota(jnp.int32, (2 * L, W), 1)
    causal = (ri % L + ri // L) > ci % L
    lo2 = ci < HD_R
    r1 = lax.broadcasted_iota(jnp.int32, (L, W), 0)
    c1 = lax.broadcasted_iota(jnp.int32, (L, W), 1)
    lo1 = c1 < HD_R
    same_head = (ri // HD_R) == (ci // HD_R)
    eye2 = (r1 == c1 % L).astype(F32)

    def blockdiag(x):
        return jnp.concatenate([jnp.where(lo1, x, jnp.zeros_like(x)), jnp.where(lo1, jnp.zeros_like(x), x)], axis=0)

    pairs = range(len(units))
    sl = [slice(p * W, (p + 1) * W) for _, p in units]
    ar = [jnp.concatenate([at[bi][:, sl[u]], rt[bi][:, sl[u]]], axis=0) for u, (bi, _) in enumerate(units)]
    bk = [jnp.concatenate([bt[bi][:, sl[u]], kt[bi][:, sl[u]]], axis=0) for u, (bi, _) in enumerate(units)]
    kb = [jnp.concatenate([kt[bi][:, sl[u]], bt[bi][:, sl[u]]], axis=0) for u, (bi, _) in enumerate(units)]
    v2 = [v_ref[bi, :, sl[u]] for u, (bi, _) in enumerate(units)]
    S2 = [sbd_ref[u] for u in pairs]
    zeros2 = jnp.zeros((2 * L, W), BF16)
    g_a = [jnp.where(causal, _dot_nt(jnp.where(lo2, ar[p], zeros2), bk[p]), 0.0) for p in pairs]
    g_b = [jnp.where(causal, _dot_nt(jnp.where(lo2, zeros2, ar[p]), kb[p]), 0.0) for p in pairs]
    gb2 = [jnp.where(lo2, g_a[p], g_b[p]) for p in pairs]
    gk2 = [jnp.where(lo2, g_b[p], g_a[p]).astype(BF16) for p in pairs]
    zv = jnp.zeros((L, W), BF16)
    vx = [jnp.concatenate([jnp.where(lo1, zv, v2[p]), jnp.where(lo1, v2[p], zv)], axis=0) for p in pairs]
    xs = [_dot_nt(ar[p], S2[p].astype(BF16)) + mm(gk2[p], vx[p]) for p in pairs]
    inv = [eye2 + g[:L] for g in gb2]
    pw = [g[:L].astype(BF16) for g in gb2]
    pw = [mm(q, blockdiag(q)) for q in pw]
    sq = 2
    while sq < L:
        pwb = [q.astype(BF16) for q in pw]
        both = [mm(jnp.concatenate([inv[p].astype(BF16), pwb[p]], axis=0), blockdiag(pwb[p])) for p in pairs]
        inv = [inv[p] + both[p][:L] for p in pairs]
        pw = [b_[L:] for b_ in both]
        sq *= 2
    ub = [mm(inv[p].astype(BF16), blockdiag(xs[p][:L].astype(BF16))).astype(BF16) for p in pairs]
    for u, (bi, _) in enumerate(units):
        o_ref[bi, :, sl[u]] = xs[u][L:] + mm(gb2[u][L:].astype(BF16), blockdiag(ub[u]))
    for u, (bi, p) in enumerate(units):
        upd = _dot_tn(jnp.concatenate([ub[u], v2[u]], axis=0), bk[u])
        s_new = (S2[u] + jnp.where(same_head, upd, 0.0)) * w_end[bi][:, sl[u]]
        sbd_ref[u] = s_new
        s_ref[bi, 2 * p] = s_new[0:HD_R, 0:HD_R]
        s_ref[bi, 2 * p + 1] = s_new[HD_R:W, HD_R:W]


def _wkv(r, lw, k, v, kk, b, s0):
    B, S, D = r.shape
    L = _tile(S, WKV_CHUNK)
    nb = _tile(B, WKV_SEQS)
    tok = pl.BlockSpec((nb, L, D), lambda b_, c: (b_, c, 0))
    st = pl.BlockSpec((nb, H_R, HD_R, HD_R), lambda b_, c: (b_, 0, 0, 0))
    return pl.pallas_call(
        _wkv_kernel, grid=(B // nb, S // L),
        in_specs=[tok] * 6 + [st], out_specs=(tok, st),
        out_shape=(jax.ShapeDtypeStruct((B, S, D), F32), jax.ShapeDtypeStruct(s0.shape, F32)),
        scratch_shapes=[pltpu.VMEM((nb * H_R // 2, 2 * HD_R, 2 * HD_R), F32)],
        compiler_params=_cparams("parallel", "arbitrary"), name="wkv_chunked",
    )(r, lw, k, v, kk, b, s0)


def _prep_weights(W):
    bf = lambda a: a.astype(BF16)
    row = lambda a: a.reshape(1, -1)
    P = dict(W)
    w_in = W["e_w_in"][0]
    n_main = 3 * D_A + 3 * D_F
    P["w_main"] = bf(w_in[:, :n_main])
    P["w_fl"] = bf(jnp.pad(w_in[:, n_main:], ((0, 0), (0, LANES - H_F))))
    P["b_f"] = jnp.pad(W["e_b_f"][0], (0, LANES - H_F)).reshape(1, LANES)
    P["w_out"] = bf(W["e_w_out"][0])
    for n in ("f_w_gate", "f_w_up", "f_w_down", "ple_gate", "ple_proj"):
        P[n] = bf(W[n])
    for n in ("r_w_r", "r_w_k", "r_w_v", "r_w_o", "r_w1", "r_w2", "r_a1", "r_a2", "r_g1", "r_g2"):
        P[n] = bf(W[n][0])
    for n in ("r_w0", "r_a0", "r_k_k", "r_k_a", "r_ln_w", "r_ln_b"):
        P[n] = row(W[n][0])
    P["r_r_k"] = W["r_r_k"][0].reshape(1, D_MODEL)
    P["r_mu"] = W["r_mu"][0]
    P["bd"] = _head_blockdiag(HD_R)
    return P


def _trunk(x, p, fox_cache, conv_prev, shift_prev, wkv_prev, P):
    B, S, D = x.shape
    n = B * S
    flat = lambda a: a.reshape(n, a.shape[-1])
    vec = lambda name, i: P[name][i].reshape(1, D)

    ya, q, k, v, kb, vx, lf, cst = _inproj(x, vec("mix_norm_pre", 0), P["w_main"], P["w_fl"], P["b_f"],
                                          conv_prev, P["e_conv_w"][0], v_transposed=fox_cache is None)
    if fox_cache is None:
        _, qc, kc = _cumsum(lf, carriers=True)
        o = _fox_prompt(q, kb, vx, qc, kc)
    else:
        ck, cv, clf = fox_cache
        Pn = ck.shape[1]
        c = _cumsum(lf, carriers=False)
        ct = jnp.swapaxes(c[:, :, :8], 1, 2)
        dt = _cumsum_lanes(jnp.swapaxes(clf, 1, 2).reshape(B * H_F, Pn)).reshape(B, H_F, Pn)
        to_t = lambda a: jnp.transpose(a, (0, 2, 3, 1))
        o = _fox_sample(q, to_t(ck), to_t(cv), dt, kb, vx, c, ct)
    tail_consts = lambda i: [vec("ffn_norm_pre", i), P["f_w_gate"], P["f_w_up"], P["f_w_down"],
                             vec("ffn_norm_post", i), vec("ple_norm", i), P["ple_gate"], P["ple_proj"]]
    p_all = p.reshape(p.shape[0], n, p.shape[-1])
    h = _tail_call(_l0_tail_kernel, 0, [flat(ya), flat(o), flat(x), p_all],
                   [P["w_out"], vec("mix_norm_post", 0)] + tail_consts(0), "l0_outproj_ffn_ple")

    consts = [vec("mix_norm_pre", 1), P["r_mu"], P["r_w_r"], P["r_w_k"], P["r_w_v"], P["r_w0"], P["r_w1"], P["r_w2"],
              P["r_a0"], P["r_a1"], P["r_a2"], P["r_g1"], P["r_g2"], P["r_k_k"], P["r_k_a"], P["bd"]]
    r, lw, km, vv, kk, bb, g, shift = _rwkv_prep(h.reshape(B, S, D), shift_prev.reshape(B, 1, D), consts)
    o1, wkv = _wkv(r, lw, km, vv, kk, bb, wkv_prev)
    h = _tail_call(_l1_tail_kernel, 1, [flat(o1), flat(r), flat(km), flat(vv), flat(g), h, p_all],
                   [P["r_ln_w"], P["r_ln_b"], P["r_r_k"], P["bd"], P["r_w_o"], vec("mix_norm_post", 1)]
                   + tail_consts(1), "l1_rwkvout_ffn_ple")

    return (h.reshape(B, S, D), k.reshape(1, B, S, H_F, HD_F), v.reshape(1, B, S, H_F, HD_F),
            lf[None, :, :, :H_F], cst[None], shift.reshape(1, B, D), wkv[None])


def kernel(x_prompt, x_sample, p_prompt, p_sample, cache_k, cache_v, cache_logf, state_conv, state_shift, state_wkv, mix_norm_pre, mix_norm_post, ffn_norm_pre, ffn_norm_post, e_w_in, e_b_f, e_conv_w, e_w_out, r_mu, r_w_r, r_w_k, r_w_v, r_w_o, r_w0, r_w1, r_w2, r_a0, r_a1, r_a2, r_g1, r_g2, r_k_k, r_k_a, r_r_k, r_ln_w, r_ln_b, f_w_gate, f_w_up, f_w_down, ple_norm, ple_gate, ple_proj):
    W = dict(mix_norm_pre=mix_norm_pre, mix_norm_post=mix_norm_post, ffn_norm_pre=ffn_norm_pre,
             ffn_norm_post=ffn_norm_post, e_w_in=e_w_in, e_b_f=e_b_f, e_conv_w=e_conv_w, e_w_out=e_w_out,
             r_mu=r_mu, r_w_r=r_w_r, r_w_k=r_w_k, r_w_v=r_w_v, r_w_o=r_w_o, r_w0=r_w0, r_w1=r_w1, r_w2=r_w2,
             r_a0=r_a0, r_a1=r_a1, r_a2=r_a2, r_g1=r_g1, r_g2=r_g2, r_k_k=r_k_k, r_k_a=r_k_a, r_r_k=r_r_k,
             r_ln_w=r_ln_w, r_ln_b=r_ln_b, f_w_gate=f_w_gate, f_w_up=f_w_up, f_w_down=f_w_down,
             ple_norm=ple_norm, ple_gate=ple_gate, ple_proj=ple_proj)
    P = _prep_weights(W)
    bp = x_prompt.shape[0]
    y_p, k_p, v_p, lf_p, c_p, sh_p, s_p = _trunk(
        x_prompt, p_prompt, None, jnp.zeros((bp, 2, D_A), F32), jnp.zeros((bp, D_MODEL), F32),
        jnp.zeros((bp, H_R, HD_R, HD_R), F32), P)
    y_s, k_s, v_s, lf_s, c_s, sh_s, s_s = _trunk(
        x_sample, p_sample, (cache_k[0], cache_v[0], cache_logf[0]), state_conv[0], state_shift[0], state_wkv[0], P)
    return (y_p, y_s, k_p, v_p, lf_p, c_p, sh_p, s_p, k_s, v_s, lf_s, c_s, sh_s, s_s)
```

```python
import functools

import jax
import jax.numpy as jnp
import numpy as np
from jax import lax
from jax.experimental import pallas as pl
from jax.experimental.pallas import tpu as pltpu

D_MODEL = 1024
D_A = 512
H_F = 8
HD_F = 64
D_F = H_F * HD_F
HD_R = 64
H_R = D_MODEL // HD_R
PLE_DIM = 256
D_FF = 2816
NORM_EPS = 1e-6
GN_EPS = 64e-5
L2_EPS = 1e-12
NEG_INF = -1e30

LOG2E = 1.4426950408889634
DECAY_SCALE = 0.6065306597126334
LANES = 128
FFN_CHUNK = 256
SAMPLE_HEADS = 2
ATTN_TILE = 1024
ATTN_Q_SUB = 256
WKV_GROUP = 2
WKV_SEQS = 2
WKV_CHUNK = 64
VMEM_LIMIT = 48 * 1024 * 1024

BF16 = jnp.bfloat16
F32 = jnp.float32


def _cparams(*sem):
    return pltpu.CompilerParams(dimension_semantics=sem, vmem_limit_bytes=VMEM_LIMIT)


def _tile(n, pref):
    t = min(n, pref)
    assert n % t == 0, (n, pref)
    return t


def _rms(x, g):
    return x * lax.rsqrt(jnp.mean(x * x, axis=-1, keepdims=True) + NORM_EPS) * g


def _bdot(a, w):
    return jnp.dot(a.astype(BF16), w, preferred_element_type=F32)


def _dot_nt(a, b, **kw):
    return lax.dot_general(a, b, (((1,), (1,)), ((), ())), preferred_element_type=F32, **kw)


def _dot_tn(a, b, **kw):
    return lax.dot_general(a, b, (((0,), (0,)), ((), ())), preferred_element_type=F32, **kw)


def _softplus(y):
    return jnp.maximum(y, 0.0) + jnp.log1p(jnp.exp(-jnp.abs(y)))


def _split3(x):
    hi = x.astype(BF16)
    r1 = x - hi.astype(F32)
    mid = r1.astype(BF16)
    lo = (r1 - mid.astype(F32)).astype(BF16)
    return hi, mid, lo


def _head_allsum(x, bd, exact=True):
    hi = x.astype(BF16)
    w = bd.shape[0]
    slabs = [jnp.dot(hi[:, t:t + w], bd, preferred_element_type=F32) for t in range(0, x.shape[1], w)]
    if exact:
        lo = (x - hi.astype(F32)).astype(BF16)
        slabs = [sb + jnp.dot(lo[:, t:t + w], bd, preferred_element_type=F32)
                 for sb, t in zip(slabs, range(0, x.shape[1], w))]
    return jnp.concatenate(slabs, axis=1)


def _head_blockdiag(hd, width=256):
    idx = jnp.arange(width) // hd
    return (idx[:, None] == idx[None, :]).astype(BF16)


def _prev_rows(first_ref, carry_ref, rows, nb, ts):
    if nb == 1:
        return carry_ref[rows]
    w = first_ref.shape[-1]
    return jnp.broadcast_to(first_ref[:, rows, :], (nb, ts, w)).reshape(nb * ts, w)


def _inproj_kernel(x_ref, g_ref, w_ref, wfl_ref, bf_ref, cprev_ref, cw_ref,
                   ya_ref, q_ref, k_ref, v_ref, kb_ref, vx_ref, lf_ref, cst_ref, carry_ref, *, v_transposed):
    nb, ts, D = x_ref.shape
    n = nb * ts
    if nb == 1:
        @pl.when(pl.program_id(1) == 0)
        def _():
            carry_ref[6:8, :] = cprev_ref[0]
        p0, p1 = carry_ref[6:7, :], carry_ref[7:8, :]
    else:
        p0 = _prev_rows(cprev_ref, None, slice(0, 1), nb, ts)
        p1 = _prev_rows(cprev_ref, None, slice(1, 2), nb, ts)

    xn = _rms(x_ref[...].reshape(n, D), g_ref[...]).astype(BF16)
    z = [jnp.dot(xn, w_ref[:, c * D_A:(c + 1) * D_A], preferred_element_type=F32) for c in range(6)]
    ax, a_b, a_c, q, k, v = z
    fl = jnp.dot(xn, wfl_ref[...], preferred_element_type=F32) + bf_ref[...]
    lf_ref[...] = (-_softplus(-fl)).reshape(nb, ts, LANES)
    u = a_c * ax
    t = lax.broadcasted_iota(jnp.int32, u.shape, 0) % ts
    um1 = jnp.where(t == 0, p1, pltpu.roll(u, 1, 0))
    um2 = jnp.where(t == 0, p0, jnp.where(t == 1, p1, pltpu.roll(u, 2, 0)))
    cu = cw_ref[0:1, :] * um2 + cw_ref[1:2, :] * um1 + cw_ref[2:3, :] * u
    ya_ref[...] = (a_b * cu).astype(BF16).reshape(nb, ts, D_A)
    u3 = u.reshape(nb, ts, D_A)
    if nb == 1:
        carry_ref[...] = u[ts - 8:ts, :]
    cst_ref[...] = u3[:, ts - 2:ts, :]
    shape = (nb, ts, D_F)
    q_ref[...] = (q * (HD_F ** -0.5 * LOG2E)).astype(BF16).reshape(shape)
    k_ref[...] = k.reshape(shape)
    v_ref[...] = v.reshape(shape)
    kb_ref[...] = k.astype(BF16).reshape(shape)
    if v_transposed:
        vx_ref[0] = v.T.astype(BF16)
    else:
        vx_ref[...] = v.astype(BF16).reshape(shape)


def _inproj(x, g, w_main, w_fl, b_f, conv_prev, conv_w, v_transposed):
    B, S, D = x.shape
    ts = _tile(S, 512)
    nb = _tile(B, 512 // ts) if ts == S and not v_transposed else 1
    tok = lambda w: pl.BlockSpec((nb, ts, w), lambda b, s: (b, s, 0))
    full = lambda a: pl.BlockSpec(a.shape, lambda b, s: (0,) * a.ndim)
    st = pl.BlockSpec((nb, 2, D_A), lambda b, s: (b, 0, 0))
    out_shape = (
        jax.ShapeDtypeStruct((B, S, D_A), BF16),
        jax.ShapeDtypeStruct((B, S, D_F), BF16),
        jax.ShapeDtypeStruct((B, S, D_F), F32),
        jax.ShapeDtypeStruct((B, S, D_F), F32),
        jax.ShapeDtypeStruct((B, S, D_F), BF16),
        jax.ShapeDtypeStruct((B, D_F, S) if v_transposed else (B, S, D_F), BF16),
        jax.ShapeDtypeStruct((B, S, LANES), F32),
        jax.ShapeDtypeStruct((B, 2, D_A), F32),
    )
    vx_spec = pl.BlockSpec((nb, D_F, ts), lambda b, s: (b, 0, s)) if v_transposed else tok(D_F)
    return pl.pallas_call(
        functools.partial(_inproj_kernel, v_transposed=v_transposed),
        grid=(B // nb, S // ts),
        in_specs=[tok(D), full(g), full(w_main), full(w_fl), full(b_f), st, full(conv_w)],
        out_specs=(tok(D_A), tok(D_F), tok(D_F), tok(D_F), tok(D_F), vx_spec, tok(LANES), st),
        out_shape=out_shape,
        scratch_shapes=[pltpu.VMEM((8, D_A), F32)],
        compiler_params=_cparams("parallel", "arbitrary"),
        name="l0_inproj_conv",
    )(x, g, w_main, w_fl, b_f, conv_prev, conv_w)


def _scan_rows(x):
    n = x.shape[0]
    row = lax.broadcasted_iota(jnp.int32, x.shape, 0)
    sh = 1
    while sh < n:
        x = x + jnp.where(row >= sh, pltpu.roll(x, sh, 0), 0.0)
        sh *= 2
    return x


def _cumsum_kernel(x_ref, o_ref):
    o_ref[...] = _scan_rows(x_ref[...]) * LOG2E


def _cumsum_lanes_kernel(x_ref, o_ref):
    x = x_ref[...]
    n = x.shape[1]
    col = lax.broadcasted_iota(jnp.int32, x.shape, 1)
    sh = 1
    while sh < n:
        x = x + jnp.where(col >= sh, pltpu.roll(x, sh, 1), 0.0)
        sh *= 2
    o_ref[...] = (x - x[:, n - 1:n]) * LOG2E


def _cumsum_lanes(x):
    spec = pl.BlockSpec(x.shape, lambda i: (0, 0))
    return pl.pallas_call(
        _cumsum_lanes_kernel, grid=(1,), in_specs=[spec], out_specs=spec,
        out_shape=jax.ShapeDtypeStruct(x.shape, F32),
        compiler_params=_cparams("arbitrary"), name="cache_logf_cumsum",
    )(x)


def _cumsum_carrier_kernel(x_ref, pq_ref, pk_ref, oq_ref, ok_ref, o_ref, qc_ref, kc_ref):
    c = _scan_rows(x_ref[...]) * LOG2E
    o_ref[...] = c
    parts = jnp.concatenate(_split3(c), axis=1)
    qc_ref[...] = (jnp.dot(parts, pq_ref[...], preferred_element_type=F32) + oq_ref[...]).astype(BF16)
    kc_ref[...] = (jnp.dot(parts, pk_ref[...], preferred_element_type=F32) + ok_ref[...]).astype(BF16)


def _carrier_lane(h, slot):
    return LANES * (h // 2) + (HD_F if h % 2 == 0 else 0) + slot


def _carrier_placement():
    pq = np.zeros((3 * LANES, D_F), np.float32)
    pk = np.zeros((3 * LANES, D_F), np.float32)
    oq = np.zeros((1, D_F), np.float32)
    ok = np.zeros((1, D_F), np.float32)
    for h in range(H_F):
        for part in range(3):
            pq[part * LANES + h, _carrier_lane(h, part)] = 1.0
            ok[0, _carrier_lane(h, part)] = 1.0
            pk[part * LANES + h, _carrier_lane(h, 3 + part)] = -1.0
            oq[0, _carrier_lane(h, 3 + part)] = 1.0
    return jnp.asarray(pq, BF16), jnp.asarray(pk, BF16), jnp.asarray(oq), jnp.asarray(ok)


def _cumsum(x, carriers):
    B, S, W = x.shape
    spec = pl.BlockSpec((None, S, W), lambda b: (b, 0, 0))
    if not carriers:
        return pl.pallas_call(
            _cumsum_kernel, grid=(B,), in_specs=[spec], out_specs=spec,
            out_shape=jax.ShapeDtypeStruct(x.shape, F32),
            compiler_params=_cparams("parallel"), name="logf_cumsum",
        )(x)
    consts = _carrier_placement()
    cspec = pl.BlockSpec((None, S, D_F), lambda b: (b, 0, 0))
    return pl.pallas_call(
        _cumsum_carrier_kernel, grid=(B,),
        in_specs=[spec] + [pl.BlockSpec(a.shape, lambda b: (0, 0)) for a in consts],
        out_specs=(spec, cspec, cspec),
        out_shape=(jax.ShapeDtypeStruct(x.shape, F32), jax.ShapeDtypeStruct((B, S, D_F), BF16),
                   jax.ShapeDtypeStruct((B, S, D_F), BF16)),
        compiler_params=_cparams("parallel"), name="logf_cumsum_carriers",
    )(x, *consts)


def _fox_prompt_kernel(q_ref, k_ref, vt_ref, qc_ref, kc_ref, o_ref, m_ref, acc_ref):
    i = pl.program_id(1)
    j = pl.program_id(2)
    tq, tk = q_ref.shape[0], k_ref.shape[0]
    tqs = min(tq, ATTN_Q_SUB)

    @pl.when(j == 0)
    def _():
        m_ref[...] = jnp.full(m_ref.shape, NEG_INF, F32)
        acc_ref[...] = jnp.zeros(acc_ref.shape, F32)

    def update(diag):
        lane_q = lax.broadcasted_iota(jnp.int32, (tq, LANES), 1)
        lane_k = lax.broadcasted_iota(jnp.int32, (tk, LANES), 1)
        row_v = lax.broadcasted_iota(jnp.int32, (LANES, tk), 0)
        for pair in range(H_F // 2):
            sl = slice(pair * LANES, (pair + 1) * LANES)
            q2, qc2, k2, kc2, vt2 = q_ref[:, sl], qc_ref[:, sl], k_ref[:, sl], kc_ref[:, sl], vt_ref[sl, :]
            own = lambda idx, e: (idx < HD_F) == (e == 0)
            qa = [jnp.where(own(lane_q, e), q2, qc2) for e in range(2)]
            ka = [jnp.where(own(lane_k, e), k2, kc2) for e in range(2)]
            va = [jnp.where(own(row_v, e), vt2, jnp.ones_like(vt2)) for e in range(2)]
            units = [(e, qs) for e in range(2) for qs in range(tq // tqs)]
            cs = [slice(qs * tqs, (qs + 1) * tqs) for _, qs in units]
            nk = [(qs + 1) * tqs if diag else tk for _, qs in units]
            st = [_dot_nt(ka[e][:nk[u]], qa[e][cs[u]]) for u, (e, _) in enumerate(units)]
            if diag:
                for u, (_, qs) in enumerate(units):
                    key = lax.broadcasted_iota(jnp.int32, (nk[u], tqs), 0)
                    qry = lax.broadcasted_iota(jnp.int32, (nk[u], tqs), 1) + qs * tqs
                    st[u] = jnp.where(key <= qry, st[u], NEG_INF)
            m_old = [m_ref[2 * pair + e, :, cs[u]] for u, (e, _) in enumerate(units)]
            m_new = [jnp.maximum(m_old[u], jnp.max(st[u], axis=0, keepdims=True)) for u in range(len(units))]
            pt = [jnp.exp2(st[u] - m_new[u]).astype(BF16) for u in range(len(units))]
            pv = [jnp.dot(va[e][:, :nk[u]], pt[u], preferred_element_type=F32) for u, (e, _) in enumerate(units)]
            for u, (e, _) in enumerate(units):
                h = 2 * pair + e
                acc_ref[h, :, cs[u]] = jnp.exp2(m_old[u] - m_new[u]) * acc_ref[h, :, cs[u]] + pv[u]
                m_ref[h, :, cs[u]] = m_new[u]

    @pl.when(j < i)
    def _():
        update(False)

    @pl.when(j == i)
    def _():
        update(True)
        for pair in range(H_F // 2):
            a, b = acc_ref[2 * pair], acc_ref[2 * pair + 1]
            ot = jnp.concatenate([a[:HD_F] / a[HD_F:HD_F + 1], b[HD_F:] / b[0:1]], axis=0)
            o_ref[:, pair * LANES:(pair + 1) * LANES] = ot.T.astype(o_ref.dtype)


def _fox_prompt(q, kb, vt, qc, kc):
    B, S, _ = q.shape
    t = _tile(S, ATTN_TILE)
    n = S // t
    qs = pl.BlockSpec((None, t, D_F), lambda b, i, j: (b, i, 0))
    ks = pl.BlockSpec((None, t, D_F), lambda b, i, j: (b, jnp.minimum(j, i), 0))
    vs = pl.BlockSpec((None, D_F, t), lambda b, i, j: (b, 0, jnp.minimum(j, i)))
    return pl.pallas_call(
        _fox_prompt_kernel, grid=(B, n, n),
        in_specs=[qs, ks, vs, qs, ks], out_specs=qs,
        out_shape=jax.ShapeDtypeStruct((B, S, D_F), BF16),
        scratch_shapes=[pltpu.VMEM((H_F, 1, t), F32), pltpu.VMEM((H_F, LANES, t), F32)],
        compiler_params=_cparams("parallel", "parallel", "arbitrary"),
        name="fox_prompt_attention",
    )(q, kb, vt, qc, kc)


def _fox_sample_kernel(q_ref, ckt_ref, cvt_ref, dt_ref, kn_ref, vn_ref, cn_ref, cnt_ref, o_ref,
                       m_ref, l_ref, acc_ref):
    j = pl.program_id(2)
    nh, T = q_ref.shape[0], q_ref.shape[1]
    heads = range(nh)

    @pl.when(j == 0)
    def _():
        m_ref[...] = jnp.full(m_ref.shape, NEG_INF, F32)
        l_ref[...] = jnp.zeros(l_ref.shape, F32)
        acc_ref[...] = jnp.zeros(acc_ref.shape, F32)

    def update(s, pv):
        m_old = [m_ref[h] for h in heads]
        m_new = [jnp.maximum(m_old[h], jnp.max(s[h], axis=-1, keepdims=True)) for h in heads]
        p = [jnp.exp2(s[h] - m_new[h]) for h in heads]
        o = [pv(h, p[h].astype(BF16)) for h in heads]
        for h in heads:
            alpha = jnp.exp2(m_old[h] - m_new[h])
            l_ref[h] = alpha * l_ref[h] + jnp.sum(p[h], axis=-1, keepdims=True)
            acc_ref[h] = alpha * acc_ref[h] + o[h]
            m_ref[h] = m_new[h]

    q = [q_ref[h] for h in heads]
    cn = [cn_ref[h] for h in heads]
    kt = [ckt_ref[h].astype(BF16) for h in heads]
    vt = [cvt_ref[h].astype(BF16) for h in heads]
    update([jnp.dot(q[h], kt[h], preferred_element_type=F32) + (cn[h] - dt_ref[h]) for h in heads],
           lambda h, p: _dot_nt(p, vt[h]))

    @pl.when(j == pl.num_programs(2) - 1)
    def _():
        causal = lax.broadcasted_iota(jnp.int32, (T, T), 0) >= lax.broadcasted_iota(jnp.int32, (T, T), 1)
        update([jnp.where(causal, _dot_nt(q[h], kn_ref[h]) + (cn[h] - cnt_ref[h]), NEG_INF) for h in heads],
               lambda h, p: jnp.dot(p, vn_ref[h], preferred_element_type=F32))
        for h in heads:
            o_ref[h] = (acc_ref[h] / l_ref[h]).astype(o_ref.dtype)


def _fox_sample(q, ckt, cvt, dt, kb, vb, cn, cnt):
    B, T, _ = q.shape
    P = ckt.shape[-1]
    tk = _tile(P, 4096)
    nh = SAMPLE_HEADS
    heads = lambda a: jnp.swapaxes(a.reshape(B, T, H_F, HD_F), 1, 2)
    new = lambda w: pl.BlockSpec((None, nh, T, w), lambda b, h, j: (b, h, 0, 0))
    cache = pl.BlockSpec((None, nh, HD_F, tk), lambda b, h, j: (b, h, 0, j))
    o = pl.pallas_call(
        _fox_sample_kernel, grid=(B, H_F // nh, P // tk),
        in_specs=[new(HD_F), cache, cache, pl.BlockSpec((None, nh, 1, tk), lambda b, h, j: (b, h, 0, j)),
                  new(HD_F), new(HD_F), new(1), pl.BlockSpec((None, nh, 1, T), lambda b, h, j: (b, h, 0, 0))],
        out_specs=new(HD_F),
        out_shape=jax.ShapeDtypeStruct((B, H_F, T, HD_F), BF16),
        scratch_shapes=[pltpu.VMEM((nh, T, 1), F32), pltpu.VMEM((nh, T, 1), F32), pltpu.VMEM((nh, T, HD_F), F32)],
        compiler_params=_cparams("parallel", "parallel", "arbitrary"),
        name="fox_sample_attention",
    )(heads(q), ckt, cvt, dt[:, :, None, :], heads(kb), heads(vb),
      jnp.swapaxes(cn[:, :, :H_F], 1, 2)[..., None], cnt[:, :, None, :])
    return jnp.swapaxes(o, 1, 2).reshape(B, T, D_F)


def _channel_tail(h, p_ref, gpre_ref, wg_ref, wu_ref, wd_ref, gpost_ref, gple_ref, wpg_ref, wpp_ref,
                  out_ref, act_ref):
    xn = _rms(h, gpre_ref[...]).astype(BF16)
    for c in range(0, D_FF, FFN_CHUNK):
        gate = jnp.dot(xn, wg_ref[:, c:c + FFN_CHUNK], preferred_element_type=F32)
        up = jnp.dot(xn, wu_ref[:, c:c + FFN_CHUNK], preferred_element_type=F32)
        act_ref[:, c:c + FFN_CHUNK] = (gate * jax.nn.sigmoid(gate) * up).astype(BF16)
    h = h + _rms(jnp.dot(act_ref[...], wd_ref[...], preferred_element_type=F32), gpost_ref[...])
    gate = jax.nn.sigmoid(_bdot(_rms(h, gple_ref[...]), wpg_ref[...]))
    out_ref[...] = h + gate * _bdot(p_ref[...], wpp_ref[...])


def _l0_tail_kernel(ya_ref, o_ref, x_ref, p_ref, wout_ref, gmix_ref, *tail):
    y = (jnp.dot(ya_ref[...], wout_ref[0:D_A, :], preferred_element_type=F32)
         + jnp.dot(o_ref[...], wout_ref[D_A:D_A + D_F, :], preferred_element_type=F32))
    _channel_tail(x_ref[...] + _rms(y, gmix_ref[...]), p_ref, *tail)


def _l1_tail_kernel(o_ref, r_ref, k_ref, v_ref, g_ref, h_ref, p_ref, lnw_ref, lnb_ref, rk_ref, bd_ref,
                    wo_ref, gmix_ref, *tail):
    bd = bd_ref[...]
    o = o_ref[...]
    d = o - _head_allsum(o, bd, exact=False) * (1.0 / HD_R)
    var = _head_allsum(d * d, bd, exact=False) * (1.0 / HD_R)
    on = d * lax.rsqrt(var + GN_EPS) * lnw_ref[...] + lnb_ref[...]
    rk = r_ref[...].astype(F32) * k_ref[...].astype(F32) * rk_ref[...]
    bonus = _head_allsum(rk, bd, exact=False) * v_ref[...].astype(F32)
    y = _bdot((on + bonus) * g_ref[...].astype(F32), wo_ref[...])
    _channel_tail(h_ref[...] + _rms(y, gmix_ref[...]), p_ref, *tail)


def _tail_call(kernel, layer, tokens, consts, name):
    n = tokens[0].shape[0]
    tm = _tile(n, 512)

    def tok(a):
        if a.ndim == 2:
            return pl.BlockSpec((tm, a.shape[1]), lambda i: (i, 0))
        return pl.BlockSpec((None, tm, a.shape[2]), lambda i: (layer, i, 0))

    def res(a):
        if a.ndim == 2:
            return pl.BlockSpec(a.shape, lambda i: (0, 0), pipeline_mode=pl.Buffered(1))
        return pl.BlockSpec((None,) + a.shape[1:], lambda i: (layer, 0, 0), pipeline_mode=pl.Buffered(1))

    return pl.pallas_call(
        kernel, grid=(n // tm,),
        in_specs=[tok(t) for t in tokens] + [res(c) for c in consts],
        out_specs=pl.BlockSpec((tm, D_MODEL), lambda i: (i, 0)),
        out_shape=jax.ShapeDtypeStruct((n, D_MODEL), F32),
        scratch_shapes=[pltpu.VMEM((tm, D_FF), BF16)],
        compiler_params=_cparams("parallel"), name=name,
    )(*tokens, *consts)


def _rwkv_prep_kernel(h_ref, sprev_ref, gpre_ref, mu_ref, wr_ref, wk_ref, wv_ref, w0_ref, w1_ref, w2_ref,
                      a0_ref, a1_ref, a2_ref, g1_ref, g2_ref, kk_ref, ka_ref, bd_ref,
                      r_out, lw_out, k_out, v_out, kk_out, b_out, g_out, shift_out, carry_ref):
    nb, ts, D = h_ref.shape
    n = nb * ts
    if nb == 1:
        @pl.when(pl.program_id(1) == 0)
        def _():
            carry_ref[7:8, :] = sprev_ref[0]
    prev = _prev_rows(sprev_ref, carry_ref, slice(7, 8) if nb == 1 else slice(0, 1), nb, ts)

    xn = _rms(h_ref[...].reshape(n, D), gpre_ref[...])
    t = lax.broadcasted_iota(jnp.int32, xn.shape, 0) % ts
    xx = jnp.where(t == 0, prev, pltpu.roll(xn, 1, 0)) - xn
    if nb == 1:
        carry_ref[...] = xn[ts - 8:ts, :]
    shift_out[...] = xn.reshape(nb, ts, D)[:, ts - 1:ts, :]
    xnb, xxb, mub = xn.astype(BF16), xx.astype(BF16), mu_ref[...].astype(BF16)
    mix = lambda n: xnb + xxb * mub[n:n + 1, :]
    r = _bdot(mix(0), wr_ref[...])
    k = _bdot(mix(2), wk_ref[...])
    v = _bdot(mix(3), wv_ref[...])
    wl = w0_ref[...] + _bdot(jnp.tanh(_bdot(mix(1), w1_ref[...])), w2_ref[...])
    a = jax.nn.sigmoid(a0_ref[...] + _bdot(_bdot(mix(4), a1_ref[...]), a2_ref[...]))
    g = _bdot(jax.nn.sigmoid(_bdot(mix(5), g1_ref[...])), g2_ref[...])
    kkr = k * kk_ref[...]
    kk = kkr * lax.rsqrt(jnp.maximum(_head_allsum(kkr * kkr, bd_ref[...]), L2_EPS * L2_EPS))
    shape = (nb, ts, D)
    r_out[...] = r.astype(BF16).reshape(shape)
    lw_out[...] = (-DECAY_SCALE * jax.nn.sigmoid(wl)).reshape(shape)
    k_out[...] = (k * (1.0 + (a - 1.0) * ka_ref[...])).astype(BF16).reshape(shape)
    v_out[...] = v.astype(BF16).reshape(shape)
    kk_out[...] = kk.astype(BF16).reshape(shape)
    b_out[...] = (kk * a).astype(BF16).reshape(shape)
    g_out[...] = g.astype(BF16).reshape(shape)


def _rwkv_prep(h, shift_prev, consts):
    B, S, D = h.shape
    ts = _tile(S, 512)
    nb = _tile(B, 512 // ts) if ts == S else 1
    tok = pl.BlockSpec((nb, ts, D), lambda b, s: (b, s, 0))
    row = pl.BlockSpec((nb, 1, D), lambda b, s: (b, 0, 0))
    full = lambda a: pl.BlockSpec(a.shape, lambda b, s: (0,) * a.ndim)
    big = lambda dt: jax.ShapeDtypeStruct((B, S, D), dt)
    return pl.pallas_call(
        _rwkv_prep_kernel, grid=(B // nb, S // ts),
        in_specs=[tok, row] + [full(c) for c in consts],
        out_specs=(tok,) * 7 + (row,),
        out_shape=(big(BF16), big(F32)) + (big(BF16),) * 5 + (jax.ShapeDtypeStruct((B, 1, D), F32),),
        scratch_shapes=[pltpu.VMEM((8, D), F32)],
        compiler_params=_cparams("parallel", "arbitrary"), name="rwkv_prep",
    )(h, shift_prev, *consts)


def _wkv_kernel(r_ref, lw_ref, k_ref, v_ref, kk_ref, b_ref, s0_ref, o_ref, s_ref, sbd_ref, *, n_chunks):
    c = pl.program_id(1)
    nb, L = r_ref.shape[0], r_ref.shape[1]
    G = WKV_GROUP
    W = G * HD_R
    assert L == HD_R and H_R % G == 0
    n_groups = H_R // G
    mm = functools.partial(jnp.dot, preferred_element_type=F32)
    units = [(bi, q) for bi in range(nb) for q in range(n_groups)]

    r1 = lax.broadcasted_iota(jnp.int32, (L, W), 0)
    blk1 = lax.broadcasted_iota(jnp.int32, (L, W), 1) // HD_R

    def blockdiag(x):
        return jnp.concatenate([jnp.where(blk1 == e, x, jnp.zeros_like(x)) for e in range(G)], axis=0)

    @pl.when(c == 0)
    def _():
        for u, (bi, q) in enumerate(units):
            rows = [jnp.concatenate([s0_ref[bi, G * q + e]] * G, axis=1) for e in range(G)]
            sbd_ref[u] = jnp.concatenate([jnp.where(blk1 == e, rows[e], 0.0) for e in range(G)], axis=0)

    at, rt, bt, kt, w_end = [], [], [], [], []
    for bi in range(nb):
        lw = lw_ref[bi]
        row = lax.broadcasted_iota(jnp.int32, lw.shape, 0)
        cum = lw
        sh = 1
        while sh < L:
            cum = cum + jnp.where(row >= sh, pltpu.roll(cum, sh, 0), 0.0)
            sh *= 2
        w_inv = jnp.exp(-cum)
        at.append((-kk_ref[bi].astype(F32) * jnp.exp(cum - lw)).astype(BF16))
        rt.append((r_ref[bi].astype(F32) * jnp.exp(cum)).astype(BF16))
        bt.append((b_ref[bi].astype(F32) * w_inv).astype(BF16))
        kt.append((k_ref[bi].astype(F32) * w_inv).astype(BF16))
        w_end.append(jnp.exp(cum[L - 1:L, :]))

    ri = lax.broadcasted_iota(jnp.int32, (2 * L, W), 0)
    ci = lax.broadcasted_iota(jnp.int32, (2 * L, W), 1)
    causal = (ri % L + ri // L) > ci % L
    blk2 = ci // HD_R
    same_head = (lax.broadcasted_iota(jnp.int32, (W, W), 0) // HD_R) == (lax.broadcasted_iota(jnp.int32, (W, W), 1) // HD_R)
    eye_g = (r1 == lax.broadcasted_iota(jnp.int32, (L, W), 1) % L).astype(F32)
    kcol = lambda e: (e + G // 2) % G
    zeros2 = jnp.zeros((2 * L, W), BF16)
    zl = jnp.zeros((L, W), BF16)

    idx = range(len(units))
    sl = [slice(q * W, (q + 1) * W) for _, q in units]
    ar = [jnp.concatenate([at[bi][:, sl[u]], rt[bi][:, sl[u]]], axis=0) for u, (bi, _) in enumerate(units)]
    btu = [bt[bi][:, sl[u]] for u, (bi, _) in enumerate(units)]
    ktu = [kt[bi][:, sl[u]] for u, (bi, _) in enumerate(units)]
    v4 = [v_ref[bi, :, sl[u]] for u, (bi, _) in enumerate(units)]
    S4 = [sbd_ref[u] for u in idx]

    def gram_rhs(u, e):
        return jnp.concatenate([btu[u] if j == e else ktu[u] if j == kcol(e) else zl for j in range(G)], axis=0)

    g = [[jnp.where(causal, _dot_nt(jnp.where(blk2 == e, ar[u], zeros2), gram_rhs(u, e)), 0.0) for e in range(G)]
         for u in idx]
    gb, gk = [], []
    for u in idx:
        b_part, k_part = g[u][0], g[u][(0 + G // 2) % G]
        for e in range(1, G):
            b_part = jnp.where(blk2 == e, g[u][e], b_part)
            k_part = jnp.where(blk2 == e, g[u][(e + G // 2) % G], k_part)
        gb.append(b_part)
        gk.append(k_part.astype(BF16))
    vx = [jnp.concatenate([jnp.where(blk1 == (j + G // 2) % G, v4[u], zl) for j in range(G)], axis=0) for u in idx]
    xs = [_dot_nt(ar[u], S4[u].astype(BF16)) + mm(gk[u], vx[u]) for u in idx]
    inv = [eye_g + x[:L] for x in gb]
    pw = [x[:L].astype(BF16) for x in gb]
    pw = [mm(x, blockdiag(x)) for x in pw]
    sq = 2
    while sq < L:
        pwb = [x.astype(BF16) for x in pw]
        both = [mm(jnp.concatenate([inv[u].astype(BF16), pwb[u]], axis=0), blockdiag(pwb[u])) for u in idx]
        inv = [inv[u] + both[u][:L] for u in idx]
        pw = [x[L:] for x in both]
        sq *= 2
    ub = [mm(inv[u].astype(BF16), blockdiag(xs[u][:L].astype(BF16))).astype(BF16) for u in idx]
    for u, (bi, _) in enumerate(units):
        o_ref[bi, :, sl[u]] = xs[u][L:] + mm(gb[u][L:].astype(BF16), blockdiag(ub[u]))
    s_new = []
    for u, (bi, _) in enumerate(units):
        upd = _dot_tn(jnp.concatenate([ub[u], v4[u]], axis=0), jnp.concatenate([btu[u], ktu[u]], axis=0))
        s_new.append((S4[u] + jnp.where(same_head, upd, 0.0)) * w_end[bi][:, sl[u]])
        sbd_ref[u] = s_new[u]

    def emit_state():
        for u, (bi, q) in enumerate(units):
            for e in range(G):
                s_ref[bi, G * q + e] = s_new[u][e * HD_R:(e + 1) * HD_R, e * HD_R:(e + 1) * HD_R]

    if n_chunks == 1:
        emit_state()
    else:
        pl.when(c == n_chunks - 1)(emit_state)


def _wkv(r, lw, k, v, kk, b, s0):
    B, S, D = r.shape
    L = _tile(S, WKV_CHUNK)
    nb = _tile(B, WKV_SEQS)
    gw = WKV_GROUP * HD_R
    tok = pl.BlockSpec((nb, L, D), lambda b_, c: (b_, c, 0))
    st = pl.BlockSpec((nb, H_R, HD_R, HD_R), lambda b_, c: (b_, 0, 0, 0))
    return pl.pallas_call(
        functools.partial(_wkv_kernel, n_chunks=S // L), grid=(B // nb, S // L),
        in_specs=[tok] * 6 + [st], out_specs=(tok, st),
        out_shape=(jax.ShapeDtypeStruct((B, S, D), F32), jax.ShapeDtypeStruct(s0.shape, F32)),
        scratch_shapes=[pltpu.VMEM((nb * H_R // WKV_GROUP, gw, gw), F32)],
        compiler_params=_cparams("parallel", "arbitrary"), name="wkv_chunked",
    )(r, lw, k, v, kk, b, s0)


def _prep_weights(W):
    bf = lambda a: a.astype(BF16)
    row = lambda a: a.reshape(1, -1)
    P = dict(W)
    w_in = W["e_w_in"][0]
    n_main = 3 * D_A + 3 * D_F
    P["w_main"] = bf(w_in[:, :n_main])
    P["w_fl"] = bf(jnp.pad(w_in[:, n_main:], ((0, 0), (0, LANES - H_F))))
    P["b_f"] = jnp.pad(W["e_b_f"][0], (0, LANES - H_F)).reshape(1, LANES)
    P["w_out"] = bf(W["e_w_out"][0])
    for n in ("f_w_gate", "f_w_up", "f_w_down", "ple_gate", "ple_proj"):
        P[n] = bf(W[n])
    for n in ("r_w_r", "r_w_k", "r_w_v", "r_w_o", "r_w1", "r_w2", "r_a1", "r_a2", "r_g1", "r_g2"):
        P[n] = bf(W[n][0])
    for n in ("r_w0", "r_a0", "r_k_k", "r_k_a", "r_ln_w", "r_ln_b"):
        P[n] = row(W[n][0])
    P["r_r_k"] = W["r_r_k"][0].reshape(1, D_MODEL)
    P["r_mu"] = W["r_mu"][0]
    P["bd"] = _head_blockdiag(HD_R)
    return P


def _trunk(x, p, fox_cache, conv_prev, shift_prev, wkv_prev, P):
    B, S, D = x.shape
    n = B * S
    flat = lambda a: a.reshape(n, a.shape[-1])
    vec = lambda name, i: P[name][i].reshape(1, D)

    ya, q, k, v, kb, vx, lf, cst = _inproj(x, vec("mix_norm_pre", 0), P["w_main"], P["w_fl"], P["b_f"],
                                          conv_prev, P["e_conv_w"][0], v_transposed=fox_cache is None)
    if fox_cache is None:
        _, qc, kc = _cumsum(lf, carriers=True)
        o = _fox_prompt(q, kb, vx, qc, kc)
    else:
        ck, cv, clf = fox_cache
        Pn = ck.shape[1]
        c = _cumsum(lf, carriers=False)
        ct = jnp.swapaxes(c[:, :, :8], 1, 2)
        dt = _cumsum_lanes(jnp.swapaxes(clf, 1, 2).reshape(B * H_F, Pn)).reshape(B, H_F, Pn)
        to_t = lambda a: jnp.transpose(a, (0, 2, 3, 1))
        o = _fox_sample(q, to_t(ck), to_t(cv), dt, kb, vx, c, ct)
    tail_consts = lambda i: [vec("ffn_norm_pre", i), P["f_w_gate"], P["f_w_up"], P["f_w_down"],
                             vec("ffn_norm_post", i), vec("ple_norm", i), P["ple_gate"], P["ple_proj"]]
    p_all = p.reshape(p.shape[0], n, p.shape[-1])
    h = _tail_call(_l0_tail_kernel, 0, [flat(ya), flat(o), flat(x), p_all],
                   [P["w_out"], vec("mix_norm_post", 0)] + tail_consts(0), "l0_outproj_ffn_ple")

    consts = [vec("mix_norm_pre", 1), P["r_mu"], P["r_w_r"], P["r_w_k"], P["r_w_v"], P["r_w0"], P["r_w1"], P["r_w2"],
              P["r_a0"], P["r_a1"], P["r_a2"], P["r_g1"], P["r_g2"], P["r_k_k"], P["r_k_a"], P["bd"]]
    r, lw, km, vv, kk, bb, g, shift = _rwkv_prep(h.reshape(B, S, D), shift_prev.reshape(B, 1, D), consts)
    o1, wkv = _wkv(r, lw, km, vv, kk, bb, wkv_prev)
    h = _tail_call(_l1_tail_kernel, 1, [flat(o1), flat(r), flat(km), flat(vv), flat(g), h, p_all],
                   [P["r_ln_w"], P["r_ln_b"], P["r_r_k"], P["bd"], P["r_w_o"], vec("mix_norm_post", 1)]
                   + tail_consts(1), "l1_rwkvout_ffn_ple")

    return (h.reshape(B, S, D), k.reshape(1, B, S, H_F, HD_F), v.reshape(1, B, S, H_F, HD_F),
            lf[None, :, :, :H_F], cst[None], shift.reshape(1, B, D), wkv[None])


def kernel(x_prompt, x_sample, p_prompt, p_sample, cache_k, cache_v, cache_logf, state_conv, state_shift, state_wkv, mix_norm_pre, mix_norm_post, ffn_norm_pre, ffn_norm_post, e_w_in, e_b_f, e_conv_w, e_w_out, r_mu, r_w_r, r_w_k, r_w_v, r_w_o, r_w0, r_w1, r_w2, r_a0, r_a1, r_a2, r_g1, r_g2, r_k_k, r_k_a, r_r_k, r_ln_w, r_ln_b, f_w_gate, f_w_up, f_w_down, ple_norm, ple_gate, ple_proj):
    W = dict(mix_norm_pre=mix_norm_pre, mix_norm_post=mix_norm_post, ffn_norm_pre=ffn_norm_pre,
             ffn_norm_post=ffn_norm_post, e_w_in=e_w_in, e_b_f=e_b_f, e_conv_w=e_conv_w, e_w_out=e_w_out,
             r_mu=r_mu, r_w_r=r_w_r, r_w_k=r_w_k, r_w_v=r_w_v, r_w_o=r_w_o, r_w0=r_w0, r_w1=r_w1, r_w2=r_w2,
             r_a0=r_a0, r_a1=r_a1, r_a2=r_a2, r_g1=r_g1, r_g2=r_g2, r_k_k=r_k_k, r_k_a=r_k_a, r_r_k=r_r_k,
             r_ln_w=r_ln_w, r_ln_b=r_ln_b, f_w_gate=f_w_gate, f_w_up=f_w_up, f_w_down=f_w_down,
             ple_norm=ple_norm, ple_gate=ple_gate, ple_proj=ple_proj)
    P = _prep_weights(W)
    bp = x_prompt.shape[0]
    y_p, k_p, v_p, lf_p, c_p, sh_p, s_p = _trunk(
        x_prompt, p_prompt, None, jnp.zeros((bp, 2, D_A), F32), jnp.zeros((bp, D_MODEL), F32),
        jnp.zeros((bp, H_R, HD_R, HD_R), F32), P)
    y_s, k_s, v_s, lf_s, c_s, sh_s, s_s = _trunk(
        x_sample, p_sample, (cache_k[0], cache_v[0], cache_logf[0]), state_conv[0], state_shift[0], state_wkv[0], P)
    return (y_p, y_s, k_p, v_p, lf_p, c_p, sh_p, s_p, k_s, v_s, lf_s, c_s, sh_s, s_s)
```

```python
import functools

import jax
import jax.numpy as jnp
import numpy as np
from jax import lax
from jax.experimental import pallas as pl
from jax.experimental.pallas import tpu as pltpu

D_MODEL = 1024
D_A = 512
H_F = 8
HD_F = 64
D_F = H_F * HD_F
HD_R = 64
H_R = D_MODEL // HD_R
PLE_DIM = 256
D_FF = 2816
NORM_EPS = 1e-6
GN_EPS = 64e-5
L2_EPS = 1e-12
NEG_INF = -1e30

LOG2E = 1.4426950408889634
DECAY_SCALE = 0.6065306597126334
LANES = 128
FFN_CHUNK = 256
SAMPLE_HEADS = 4
ATTN_TILE = 1024
ATTN_Q_SUB = 256
WKV_GROUP = 2
WKV_SEQS = 4
WKV_CHUNK = 64
VMEM_LIMIT = 48 * 1024 * 1024

BF16 = jnp.bfloat16
F32 = jnp.float32


def _cparams(*sem):
    return pltpu.CompilerParams(dimension_semantics=sem, vmem_limit_bytes=VMEM_LIMIT)


def _tile(n, pref):
    t = min(n, pref)
    assert n % t == 0, (n, pref)
    return t


def _rms(x, g):
    return x * lax.rsqrt(jnp.mean(x * x, axis=-1, keepdims=True) + NORM_EPS) * g


def _bdot(a, w):
    return jnp.dot(a.astype(BF16), w, preferred_element_type=F32)


def _dot_nt(a, b, **kw):
    return lax.dot_general(a, b, (((1,), (1,)), ((), ())), preferred_element_type=F32, **kw)


def _dot_tn(a, b, **kw):
    return lax.dot_general(a, b, (((0,), (0,)), ((), ())), preferred_element_type=F32, **kw)


def _softplus(y):
    return jnp.maximum(y, 0.0) + jnp.log1p(jnp.exp(-jnp.abs(y)))


def _split3(x):
    hi = x.astype(BF16)
    r1 = x - hi.astype(F32)
    mid = r1.astype(BF16)
    lo = (r1 - mid.astype(F32)).astype(BF16)
    return hi, mid, lo


def _head_allsum(x, bd, exact=True):
    hi = x.astype(BF16)
    w = bd.shape[0]
    slabs = [jnp.dot(hi[:, t:t + w], bd, preferred_element_type=F32) for t in range(0, x.shape[1], w)]
    if exact:
        lo = (x - hi.astype(F32)).astype(BF16)
        slabs = [sb + jnp.dot(lo[:, t:t + w], bd, preferred_element_type=F32)
                 for sb, t in zip(slabs, range(0, x.shape[1], w))]
    return jnp.concatenate(slabs, axis=1)


def _head_blockdiag(hd, width=256):
    idx = jnp.arange(width) // hd
    return (idx[:, None] == idx[None, :]).astype(BF16)


def _prev_rows(first_ref, carry_ref, rows, nb, ts):
    if nb == 1:
        return carry_ref[rows]
    w = first_ref.shape[-1]
    return jnp.broadcast_to(first_ref[:, rows, :], (nb, ts, w)).reshape(nb * ts, w)


def _inproj_kernel(x_ref, g_ref, w_ref, wfl_ref, bf_ref, cprev_ref, cw_ref,
                   ya_ref, q_ref, k_ref, v_ref, kb_ref, vx_ref, lf_ref, cst_ref, carry_ref, *, v_transposed):
    nb, ts, D = x_ref.shape
    n = nb * ts
    if nb == 1:
        @pl.when(pl.program_id(1) == 0)
        def _():
            carry_ref[6:8, :] = cprev_ref[0]
        p0, p1 = carry_ref[6:7, :], carry_ref[7:8, :]
    else:
        p0 = _prev_rows(cprev_ref, None, slice(0, 1), nb, ts)
        p1 = _prev_rows(cprev_ref, None, slice(1, 2), nb, ts)

    xn = _rms(x_ref[...].reshape(n, D), g_ref[...]).astype(BF16)
    z = [jnp.dot(xn, w_ref[:, c * D_A:(c + 1) * D_A], preferred_element_type=F32) for c in range(6)]
    ax, a_b, a_c, q, k, v = z
    fl = jnp.dot(xn, wfl_ref[...], preferred_element_type=F32) + bf_ref[...]
    lf_ref[...] = (-_softplus(-fl)).reshape(nb, ts, LANES)
    u = a_c * ax
    t = lax.broadcasted_iota(jnp.int32, u.shape, 0) % ts
    um1 = jnp.where(t == 0, p1, pltpu.roll(u, 1, 0))
    um2 = jnp.where(t == 0, p0, jnp.where(t == 1, p1, pltpu.roll(u, 2, 0)))
    cu = cw_ref[0:1, :] * um2 + cw_ref[1:2, :] * um1 + cw_ref[2:3, :] * u
    ya_ref[...] = (a_b * cu).astype(BF16).reshape(nb, ts, D_A)
    u3 = u.reshape(nb, ts, D_A)
    if nb == 1:
        carry_ref[...] = u[ts - 8:ts, :]
    cst_ref[...] = u3[:, ts - 2:ts, :]
    shape = (nb, ts, D_F)
    q_ref[...] = (q * (HD_F ** -0.5 * LOG2E)).astype(BF16).reshape(shape)
    k_ref[...] = k.reshape(shape)
    v_ref[...] = v.reshape(shape)
    kb_ref[...] = k.astype(BF16).reshape(shape)
    if v_transposed:
        vx_ref[0] = v.T.astype(BF16)
    else:
        vx_ref[...] = v.astype(BF16).reshape(shape)


def _inproj(x, g, w_main, w_fl, b_f, conv_prev, conv_w, v_transposed):
    B, S, D = x.shape
    ts = _tile(S, 512)
    nb = _tile(B, 512 // ts) if ts == S and not v_transposed else 1
    tok = lambda w: pl.BlockSpec((nb, ts, w), lambda b, s: (b, s, 0))
    full = lambda a: pl.BlockSpec(a.shape, lambda b, s: (0,) * a.ndim)
    st = pl.BlockSpec((nb, 2, D_A), lambda b, s: (b, 0, 0))
    out_shape = (
        jax.ShapeDtypeStruct((B, S, D_A), BF16),
        jax.ShapeDtypeStruct((B, S, D_F), BF16),
        jax.ShapeDtypeStruct((B, S, D_F), F32),
        jax.ShapeDtypeStruct((B, S, D_F), F32),
        jax.ShapeDtypeStruct((B, S, D_F), BF16),
        jax.ShapeDtypeStruct((B, D_F, S) if v_transposed else (B, S, D_F), BF16),
        jax.ShapeDtypeStruct((B, S, LANES), F32),
        jax.ShapeDtypeStruct((B, 2, D_A), F32),
    )
    vx_spec = pl.BlockSpec((nb, D_F, ts), lambda b, s: (b, 0, s)) if v_transposed else tok(D_F)
    return pl.pallas_call(
        functools.partial(_inproj_kernel, v_transposed=v_transposed),
        grid=(B // nb, S // ts),
        in_specs=[tok(D), full(g), full(w_main), full(w_fl), full(b_f), st, full(conv_w)],
        out_specs=(tok(D_A), tok(D_F), tok(D_F), tok(D_F), tok(D_F), vx_spec, tok(LANES), st),
        out_shape=out_shape,
        scratch_shapes=[pltpu.VMEM((8, D_A), F32)],
        compiler_params=_cparams("parallel", "arbitrary"),
        name="l0_inproj_conv",
    )(x, g, w_main, w_fl, b_f, conv_prev, conv_w)


def _scan_rows(x):
    n = x.shape[0]
    row = lax.broadcasted_iota(jnp.int32, x.shape, 0)
    sh = 1
    while sh < n:
        x = x + jnp.where(row >= sh, pltpu.roll(x, sh, 0), 0.0)
        sh *= 2
    return x


def _cumsum_kernel(x_ref, o_ref):
    o_ref[...] = _scan_rows(x_ref[...]) * LOG2E


def _cumsum_lanes_kernel(x_ref, o_ref):
    x = x_ref[...]
    n = x.shape[1]
    col = lax.broadcasted_iota(jnp.int32, x.shape, 1)
    sh = 1
    while sh < n:
        x = x + jnp.where(col >= sh, pltpu.roll(x, sh, 1), 0.0)
        sh *= 2
    o_ref[...] = (x - x[:, n - 1:n]) * LOG2E


def _cumsum_lanes(x):
    spec = pl.BlockSpec(x.shape, lambda i: (0, 0))
    return pl.pallas_call(
        _cumsum_lanes_kernel, grid=(1,), in_specs=[spec], out_specs=spec,
        out_shape=jax.ShapeDtypeStruct(x.shape, F32),
        compiler_params=_cparams("arbitrary"), name="cache_logf_cumsum",
    )(x)


def _cumsum_carrier_kernel(x_ref, pq_ref, pk_ref, oq_ref, ok_ref, o_ref, qc_ref, kc_ref):
    c = _scan_rows(x_ref[...]) * LOG2E
    o_ref[...] = c
    parts = jnp.concatenate(_split3(c), axis=1)
    qc_ref[...] = (jnp.dot(parts, pq_ref[...], preferred_element_type=F32) + oq_ref[...]).astype(BF16)
    kc_ref[...] = (jnp.dot(parts, pk_ref[...], preferred_element_type=F32) + ok_ref[...]).astype(BF16)


def _carrier_lane(h, slot):
    return LANES * (h // 2) + (HD_F if h % 2 == 0 else 0) + slot


def _carrier_placement():
    pq = np.zeros((3 * LANES, D_F), np.float32)
    pk = np.zeros((3 * LANES, D_F), np.float32)
    oq = np.zeros((1, D_F), np.float32)
    ok = np.zeros((1, D_F), np.float32)
    for h in range(H_F):
        for part in range(3):
            pq[part * LANES + h, _carrier_lane(h, part)] = 1.0
            ok[0, _carrier_lane(h, part)] = 1.0
            pk[part * LANES + h, _carrier_lane(h, 3 + part)] = -1.0
            oq[0, _carrier_lane(h, 3 + part)] = 1.0
    return jnp.asarray(pq, BF16), jnp.asarray(pk, BF16), jnp.asarray(oq), jnp.asarray(ok)


def _cumsum(x, carriers):
    B, S, W = x.shape
    spec = pl.BlockSpec((None, S, W), lambda b: (b, 0, 0))
    if not carriers:
        return pl.pallas_call(
            _cumsum_kernel, grid=(B,), in_specs=[spec], out_specs=spec,
            out_shape=jax.ShapeDtypeStruct(x.shape, F32),
            compiler_params=_cparams("parallel"), name="logf_cumsum",
        )(x)
    consts = _carrier_placement()
    cspec = pl.BlockSpec((None, S, D_F), lambda b: (b, 0, 0))
    return pl.pallas_call(
        _cumsum_carrier_kernel, grid=(B,),
        in_specs=[spec] + [pl.BlockSpec(a.shape, lambda b: (0, 0)) for a in consts],
        out_specs=(spec, cspec, cspec),
        out_shape=(jax.ShapeDtypeStruct(x.shape, F32), jax.ShapeDtypeStruct((B, S, D_F), BF16),
                   jax.ShapeDtypeStruct((B, S, D_F), BF16)),
        compiler_params=_cparams("parallel"), name="logf_cumsum_carriers",
    )(x, *consts)


def _fox_prompt_kernel(q_ref, k_ref, vt_ref, qc_ref, kc_ref, o_ref, m_ref, acc_ref):
    i = pl.program_id(1)
    j = pl.program_id(2)
    tq, tk = q_ref.shape[0], k_ref.shape[0]
    tqs = min(tq, ATTN_Q_SUB)

    @pl.when(j == 0)
    def _():
        m_ref[...] = jnp.full(m_ref.shape, NEG_INF, F32)
        acc_ref[...] = jnp.zeros(acc_ref.shape, F32)

    def update(diag):
        lane_q = lax.broadcasted_iota(jnp.int32, (tq, LANES), 1)
        lane_k = lax.broadcasted_iota(jnp.int32, (tk, LANES), 1)
        row_v = lax.broadcasted_iota(jnp.int32, (LANES, tk), 0)
        for pair in range(H_F // 2):
            sl = slice(pair * LANES, (pair + 1) * LANES)
            q2, qc2, k2, kc2, vt2 = q_ref[:, sl], qc_ref[:, sl], k_ref[:, sl], kc_ref[:, sl], vt_ref[sl, :]
            own = lambda idx, e: (idx < HD_F) == (e == 0)
            qa = [jnp.where(own(lane_q, e), q2, qc2) for e in range(2)]
            ka = [jnp.where(own(lane_k, e), k2, kc2) for e in range(2)]
            va = [jnp.where(own(row_v, e), vt2, jnp.ones_like(vt2)) for e in range(2)]
            units = [(e, qs) for e in range(2) for qs in range(tq // tqs)]
            cs = [slice(qs * tqs, (qs + 1) * tqs) for _, qs in units]
            nk = [(qs + 1) * tqs if diag else tk for _, qs in units]
            st = [_dot_nt(ka[e][:nk[u]], qa[e][cs[u]]) for u, (e, _) in enumerate(units)]
            if diag:
                for u, (_, qs) in enumerate(units):
                    key = lax.broadcasted_iota(jnp.int32, (nk[u], tqs), 0)
                    qry = lax.broadcasted_iota(jnp.int32, (nk[u], tqs), 1) + qs * tqs
                    st[u] = jnp.where(key <= qry, st[u], NEG_INF)
            m_old = [m_ref[2 * pair + e, :, cs[u]] for u, (e, _) in enumerate(units)]
            m_new = [jnp.maximum(m_old[u], jnp.max(st[u], axis=0, keepdims=True)) for u in range(len(units))]
            pt = [jnp.exp2(st[u] - m_new[u]).astype(BF16) for u in range(len(units))]
            pv = [jnp.dot(va[e][:, :nk[u]], pt[u], preferred_element_type=F32) for u, (e, _) in enumerate(units)]
            for u, (e, _) in enumerate(units):
                h = 2 * pair + e
                acc_ref[h, :, cs[u]] = jnp.exp2(m_old[u] - m_new[u]) * acc_ref[h, :, cs[u]] + pv[u]
                m_ref[h, :, cs[u]] = m_new[u]

    @pl.when(j < i)
    def _():
        update(False)

    @pl.when(j == i)
    def _():
        update(True)
        for pair in range(H_F // 2):
            a, b = acc_ref[2 * pair], acc_ref[2 * pair + 1]
            ot = jnp.concatenate([a[:HD_F] / a[HD_F:HD_F + 1], b[HD_F:] / b[0:1]], axis=0)
            o_ref[:, pair * LANES:(pair + 1) * LANES] = ot.T.astype(o_ref.dtype)


def _fox_prompt(q, kb, vt, qc, kc):
    B, S, _ = q.shape
    t = _tile(S, ATTN_TILE)
    n = S // t
    qs = pl.BlockSpec((None, t, D_F), lambda b, i, j: (b, i, 0))
    ks = pl.BlockSpec((None, t, D_F), lambda b, i, j: (b, jnp.minimum(j, i), 0))
    vs = pl.BlockSpec((None, D_F, t), lambda b, i, j: (b, 0, jnp.minimum(j, i)))
    return pl.pallas_call(
        _fox_prompt_kernel, grid=(B, n, n),
        in_specs=[qs, ks, vs, qs, ks], out_specs=qs,
        out_shape=jax.ShapeDtypeStruct((B, S, D_F), BF16),
        scratch_shapes=[pltpu.VMEM((H_F, 1, t), F32), pltpu.VMEM((H_F, LANES, t), F32)],
        compiler_params=_cparams("parallel", "parallel", "arbitrary"),
        name="fox_prompt_attention",
    )(q, kb, vt, qc, kc)


def _fox_sample_kernel(q_ref, ckt_ref, cvt_ref, dt_ref, kn_ref, vn_ref, cn_ref, cnt_ref, o_ref,
                       m_ref, l_ref, acc_ref):
    j = pl.program_id(2)
    nh, T = q_ref.shape[0], q_ref.shape[1]
    heads = range(nh)

    @pl.when(j == 0)
    def _():
        m_ref[...] = jnp.full(m_ref.shape, NEG_INF, F32)
        l_ref[...] = jnp.zeros(l_ref.shape, F32)
        acc_ref[...] = jnp.zeros(acc_ref.shape, F32)

    def update(s, pv):
        m_old = [m_ref[h] for h in heads]
        m_new = [jnp.maximum(m_old[h], jnp.max(s[h], axis=-1, keepdims=True)) for h in heads]
        p = [jnp.exp2(s[h] - m_new[h]) for h in heads]
        o = [pv(h, p[h].astype(BF16)) for h in heads]
        for h in heads:
            alpha = jnp.exp2(m_old[h] - m_new[h])
            l_ref[h] = alpha * l_ref[h] + jnp.sum(p[h], axis=-1, keepdims=True)
            acc_ref[h] = alpha * acc_ref[h] + o[h]
            m_ref[h] = m_new[h]

    q = [q_ref[h] for h in heads]
    cn = [cn_ref[h] for h in heads]
    kt = [ckt_ref[h].astype(BF16) for h in heads]
    vt = [cvt_ref[h].astype(BF16) for h in heads]
    update([jnp.dot(q[h], kt[h], preferred_element_type=F32) + (cn[h] - dt_ref[h]) for h in heads],
           lambda h, p: _dot_nt(p, vt[h]))

    @pl.when(j == pl.num_programs(2) - 1)
    def _():
        causal = lax.broadcasted_iota(jnp.int32, (T, T), 0) >= lax.broadcasted_iota(jnp.int32, (T, T), 1)
        update([jnp.where(causal, _dot_nt(q[h], kn_ref[h]) + (cn[h] - cnt_ref[h]), NEG_INF) for h in heads],
               lambda h, p: jnp.dot(p, vn_ref[h], preferred_element_type=F32))
        for h in heads:
            o_ref[h] = (acc_ref[h] / l_ref[h]).astype(o_ref.dtype)


def _fox_sample(q, ckt, cvt, dt, kb, vb, cn, cnt):
    B, T, _ = q.shape
    P = ckt.shape[-1]
    tk = _tile(P, 4096)
    nh = SAMPLE_HEADS
    heads = lambda a: jnp.swapaxes(a.reshape(B, T, H_F, HD_F), 1, 2)
    new = lambda w: pl.BlockSpec((None, nh, T, w), lambda b, h, j: (b, h, 0, 0))
    cache = pl.BlockSpec((None, nh, HD_F, tk), lambda b, h, j: (b, h, 0, j))
    o = pl.pallas_call(
        _fox_sample_kernel, grid=(B, H_F // nh, P // tk),
        in_specs=[new(HD_F), cache, cache, pl.BlockSpec((None, nh, 1, tk), lambda b, h, j: (b, h, 0, j)),
                  new(HD_F), new(HD_F), new(1), pl.BlockSpec((None, nh, 1, T), lambda b, h, j: (b, h, 0, 0))],
        out_specs=new(HD_F),
        out_shape=jax.ShapeDtypeStruct((B, H_F, T, HD_F), BF16),
        scratch_shapes=[pltpu.VMEM((nh, T, 1), F32), pltpu.VMEM((nh, T, 1), F32), pltpu.VMEM((nh, T, HD_F), F32)],
        compiler_params=_cparams("parallel", "parallel", "arbitrary"),
        name="fox_sample_attention",
    )(heads(q), ckt, cvt, dt[:, :, None, :], heads(kb), heads(vb),
      jnp.swapaxes(cn[:, :, :H_F], 1, 2)[..., None], cnt[:, :, None, :])
    return jnp.swapaxes(o, 1, 2).reshape(B, T, D_F)


def _channel_tail(h, p_ref, gpre_ref, wg_ref, wu_ref, wd_ref, gpost_ref, gple_ref, wpg_ref, wpp_ref,
                  out_ref, act_ref):
    xn = _rms(h, gpre_ref[...]).astype(BF16)
    for c in range(0, D_FF, FFN_CHUNK):
        gate = jnp.dot(xn, wg_ref[:, c:c + FFN_CHUNK], preferred_element_type=F32)
        up = jnp.dot(xn, wu_ref[:, c:c + FFN_CHUNK], preferred_element_type=F32)
        act_ref[:, c:c + FFN_CHUNK] = (gate * jax.nn.sigmoid(gate) * up).astype(BF16)
    h = h + _rms(jnp.dot(act_ref[...], wd_ref[...], preferred_element_type=F32), gpost_ref[...])
    gate = jax.nn.sigmoid(_bdot(_rms(h, gple_ref[...]), wpg_ref[...]))
    out_ref[...] = h + gate * _bdot(p_ref[...], wpp_ref[...])


def _l0_tail_kernel(ya_ref, o_ref, x_ref, p_ref, wout_ref, gmix_ref, *tail):
    y = (jnp.dot(ya_ref[...], wout_ref[0:D_A, :], preferred_element_type=F32)
         + jnp.dot(o_ref[...], wout_ref[D_A:D_A + D_F, :], preferred_element_type=F32))
    _channel_tail(x_ref[...] + _rms(y, gmix_ref[...]), p_ref, *tail)


def _l1_tail_kernel(o_ref, r_ref, k_ref, v_ref, g_ref, h_ref, p_ref, lnw_ref, lnb_ref, rk_ref, bd_ref,
                    wo_ref, gmix_ref, *tail):
    bd = bd_ref[...]
    o = o_ref[...]
    d = o - _head_allsum(o, bd, exact=False) * (1.0 / HD_R)
    var = _head_allsum(d * d, bd, exact=False) * (1.0 / HD_R)
    on = d * lax.rsqrt(var + GN_EPS) * lnw_ref[...] + lnb_ref[...]
    rk = r_ref[...].astype(F32) * k_ref[...].astype(F32) * rk_ref[...]
    bonus = _head_allsum(rk, bd, exact=False) * v_ref[...].astype(F32)
    y = _bdot((on + bonus) * g_ref[...].astype(F32), wo_ref[...])
    _channel_tail(h_ref[...] + _rms(y, gmix_ref[...]), p_ref, *tail)


def _tail_call(kernel, layer, tokens, consts, name):
    n = tokens[0].shape[0]
    tm = _tile(n, 512)

    def tok(a):
        if a.ndim == 2:
            return pl.BlockSpec((tm, a.shape[1]), lambda i: (i, 0))
        return pl.BlockSpec((None, tm, a.shape[2]), lambda i: (layer, i, 0))

    def res(a):
        if a.ndim == 2:
            return pl.BlockSpec(a.shape, lambda i: (0, 0), pipeline_mode=pl.Buffered(1))
        return pl.BlockSpec((None,) + a.shape[1:], lambda i: (layer, 0, 0), pipeline_mode=pl.Buffered(1))

    return pl.pallas_call(
        kernel, grid=(n // tm,),
        in_specs=[tok(t) for t in tokens] + [res(c) for c in consts],
        out_specs=pl.BlockSpec((tm, D_MODEL), lambda i: (i, 0)),
        out_shape=jax.ShapeDtypeStruct((n, D_MODEL), F32),
        scratch_shapes=[pltpu.VMEM((tm, D_FF), BF16)],
        compiler_params=_cparams("parallel"), name=name,
    )(*tokens, *consts)


def _rwkv_prep_kernel(h_ref, sprev_ref, gpre_ref, mu_ref, wr_ref, wk_ref, wv_ref, w0_ref, w1_ref, w2_ref,
                      a0_ref, a1_ref, a2_ref, g1_ref, g2_ref, kk_ref, ka_ref, bd_ref,
                      r_out, lw_out, k_out, v_out, kk_out, b_out, g_out, shift_out, carry_ref):
    nb, ts, D = h_ref.shape
    n = nb * ts
    if nb == 1:
        @pl.when(pl.program_id(1) == 0)
        def _():
            carry_ref[7:8, :] = sprev_ref[0]
    prev = _prev_rows(sprev_ref, carry_ref, slice(7, 8) if nb == 1 else slice(0, 1), nb, ts)

    xn = _rms(h_ref[...].reshape(n, D), gpre_ref[...])
    t = lax.broadcasted_iota(jnp.int32, xn.shape, 0) % ts
    xx = jnp.where(t == 0, prev, pltpu.roll(xn, 1, 0)) - xn
    if nb == 1:
        carry_ref[...] = xn[ts - 8:ts, :]
    shift_out[...] = xn.reshape(nb, ts, D)[:, ts - 1:ts, :]
    xnb, xxb, mub = xn.astype(BF16), xx.astype(BF16), mu_ref[...].astype(BF16)
    mix = lambda n: xnb + xxb * mub[n:n + 1, :]
    r = _bdot(mix(0), wr_ref[...])
    k = _bdot(mix(2), wk_ref[...])
    v = _bdot(mix(3), wv_ref[...])
    wl = w0_ref[...] + _bdot(jnp.tanh(_bdot(mix(1), w1_ref[...])), w2_ref[...])
    a = jax.nn.sigmoid(a0_ref[...] + _bdot(_bdot(mix(4), a1_ref[...]), a2_ref[...]))
    g = _bdot(jax.nn.sigmoid(_bdot(mix(5), g1_ref[...])), g2_ref[...])
    kkr = k * kk_ref[...]
    kk = kkr * lax.rsqrt(jnp.maximum(_head_allsum(kkr * kkr, bd_ref[...]), L2_EPS * L2_EPS))
    shape = (nb, ts, D)
    r_out[...] = r.astype(BF16).reshape(shape)
    lw_out[...] = (-DECAY_SCALE * jax.nn.sigmoid(wl)).reshape(shape)
    k_out[...] = (k * (1.0 + (a - 1.0) * ka_ref[...])).astype(BF16).reshape(shape)
    v_out[...] = v.astype(BF16).reshape(shape)
    kk_out[...] = kk.astype(BF16).reshape(shape)
    b_out[...] = (kk * a).astype(BF16).reshape(shape)
    g_out[...] = g.astype(BF16).reshape(shape)


def _rwkv_prep(h, shift_prev, consts):
    B, S, D = h.shape
    ts = _tile(S, 512)
    nb = _tile(B, 512 // ts) if ts == S else 1
    tok = pl.BlockSpec((nb, ts, D), lambda b, s: (b, s, 0))
    row = pl.BlockSpec((nb, 1, D), lambda b, s: (b, 0, 0))
    full = lambda a: pl.BlockSpec(a.shape, lambda b, s: (0,) * a.ndim)
    big = lambda dt: jax.ShapeDtypeStruct((B, S, D), dt)
    return pl.pallas_call(
        _rwkv_prep_kernel, grid=(B // nb, S // ts),
        in_specs=[tok, row] + [full(c) for c in consts],
        out_specs=(tok,) * 7 + (row,),
        out_shape=(big(BF16), big(F32)) + (big(BF16),) * 5 + (jax.ShapeDtypeStruct((B, 1, D), F32),),
        scratch_shapes=[pltpu.VMEM((8, D), F32)],
        compiler_params=_cparams("parallel", "arbitrary"), name="rwkv_prep",
    )(h, shift_prev, *consts)


def _wkv_kernel(r_ref, lw_ref, k_ref, v_ref, kk_ref, b_ref, s0_ref, o_ref, s_ref, sbd_ref, *, n_chunks):
    c = pl.program_id(1)
    nb, L = r_ref.shape[0], r_ref.shape[1]
    G = WKV_GROUP
    W = G * HD_R
    assert L == HD_R and H_R % G == 0
    n_groups = H_R // G
    mm = functools.partial(jnp.dot, preferred_element_type=F32)
    units = [(bi, q) for bi in range(nb) for q in range(n_groups)]

    r1 = lax.broadcasted_iota(jnp.int32, (L, W), 0)
    blk1 = lax.broadcasted_iota(jnp.int32, (L, W), 1) // HD_R

    def blockdiag(x):
        return jnp.concatenate([jnp.where(blk1 == e, x, jnp.zeros_like(x)) for e in range(G)], axis=0)

    @pl.when(c == 0)
    def _():
        for u, (bi, q) in enumerate(units):
            rows = [jnp.concatenate([s0_ref[bi, G * q + e]] * G, axis=1) for e in range(G)]
            sbd_ref[u] = jnp.concatenate([jnp.where(blk1 == e, rows[e], 0.0) for e in range(G)], axis=0)

    at, rt, bt, kt, w_end = [], [], [], [], []
    for bi in range(nb):
        lw = lw_ref[bi]
        row = lax.broadcasted_iota(jnp.int32, lw.shape, 0)
        cum = lw
        sh = 1
        while sh < L:
            cum = cum + jnp.where(row >= sh, pltpu.roll(cum, sh, 0), 0.0)
            sh *= 2
        w_inv = jnp.exp(-cum)
        at.append((-kk_ref[bi].astype(F32) * jnp.exp(cum - lw)).astype(BF16))
        rt.append((r_ref[bi].astype(F32) * jnp.exp(cum)).astype(BF16))
        bt.append((b_ref[bi].astype(F32) * w_inv).astype(BF16))
        kt.append((k_ref[bi].astype(F32) * w_inv).astype(BF16))
        w_end.append(jnp.exp(cum[L - 1:L, :]))

    ri = lax.broadcasted_iota(jnp.int32, (2 * L, W), 0)
    ci = lax.broadcasted_iota(jnp.int32, (2 * L, W), 1)
    causal = (ri % L + ri // L) > ci % L
    blk2 = ci // HD_R
    same_head = (lax.broadcasted_iota(jnp.int32, (W, W), 0) // HD_R) == (lax.broadcasted_iota(jnp.int32, (W, W), 1) // HD_R)
    eye_g = (r1 == lax.broadcasted_iota(jnp.int32, (L, W), 1) % L).astype(F32)
    kcol = lambda e: (e + G // 2) % G
    zeros2 = jnp.zeros((2 * L, W), BF16)
    zl = jnp.zeros((L, W), BF16)

    idx = range(len(units))
    sl = [slice(q * W, (q + 1) * W) for _, q in units]
    ar = [jnp.concatenate([at[bi][:, sl[u]], rt[bi][:, sl[u]]], axis=0) for u, (bi, _) in enumerate(units)]
    btu = [bt[bi][:, sl[u]] for u, (bi, _) in enumerate(units)]
    ktu = [kt[bi][:, sl[u]] for u, (bi, _) in enumerate(units)]
    v4 = [v_ref[bi, :, sl[u]] for u, (bi, _) in enumerate(units)]
    S4 = [sbd_ref[u] for u in idx]

    def gram_rhs(u, e):
        return jnp.concatenate([btu[u] if j == e else ktu[u] if j == kcol(e) else zl for j in range(G)], axis=0)

    g = [[jnp.where(causal, _dot_nt(jnp.where(blk2 == e, ar[u], zeros2), gram_rhs(u, e)), 0.0) for e in range(G)]
         for u in idx]
    gb, gk = [], []
    for u in idx:
        b_part, k_part = g[u][0], g[u][(0 + G // 2) % G]
        for e in range(1, G):
            b_part = jnp.where(blk2 == e, g[u][e], b_part)
            k_part = jnp.where(blk2 == e, g[u][(e + G // 2) % G], k_part)
        gb.append(b_part)
        gk.append(k_part.astype(BF16))
    vx = [jnp.concatenate([jnp.where(blk1 == (j + G // 2) % G, v4[u], zl) for j in range(G)], axis=0) for u in idx]
    xs = [_dot_nt(ar[u], S4[u].astype(BF16)) + mm(gk[u], vx[u]) for u in idx]
    inv = [eye_g + x[:L] for x in gb]
    pw = [x[:L].astype(BF16) for x in gb]
    pw = [mm(x, blockdiag(x)) for x in pw]
    sq = 2
    while sq < L:
        pwb = [x.astype(BF16) for x in pw]
        both = [mm(jnp.concatenate([inv[u].astype(BF16), pwb[u]], axis=0), blockdiag(pwb[u])) for u in idx]
        inv = [inv[u] + both[u][:L] for u in idx]
        pw = [x[L:] for x in both]
        sq *= 2
    ub = [mm(inv[u].astype(BF16), blockdiag(xs[u][:L].astype(BF16))).astype(BF16) for u in idx]
    for u, (bi, _) in enumerate(units):
        o_ref[bi, :, sl[u]] = xs[u][L:] + mm(gb[u][L:].astype(BF16), blockdiag(ub[u]))
    s_new = []
    for u, (bi, _) in enumerate(units):
        upd = _dot_tn(jnp.concatenate([ub[u], v4[u]], axis=0), jnp.concatenate([btu[u], ktu[u]], axis=0))
        s_new.append((S4[u] + jnp.where(same_head, upd, 0.0)) * w_end[bi][:, sl[u]])
        sbd_ref[u] = s_new[u]

    def emit_state():
        for u, (bi, q) in enumerate(units):
            for e in range(G):
                s_ref[bi, G * q + e] = s_new[u][e * HD_R:(e + 1) * HD_R, e * HD_R:(e + 1) * HD_R]

    if n_chunks == 1:
        emit_state()
    else:
        pl.when(c == n_chunks - 1)(emit_state)


def _wkv(r, lw, k, v, kk, b, s0):
    B, S, D = r.shape
    L = _tile(S, WKV_CHUNK)
    nb = _tile(B, WKV_SEQS)
    gw = WKV_GROUP * HD_R
    tok = pl.BlockSpec((nb, L, D), lambda b_, c: (b_, c, 0))
    st = pl.BlockSpec((nb, H_R, HD_R, HD_R), lambda b_, c: (b_, 0, 0, 0))
    return pl.pallas_call(
        functools.partial(_wkv_kernel, n_chunks=S // L), grid=(B // nb, S // L),
        in_specs=[tok] * 6 + [st], out_specs=(tok, st),
        out_shape=(jax.ShapeDtypeStruct((B, S, D), F32), jax.ShapeDtypeStruct(s0.shape, F32)),
        scratch_shapes=[pltpu.VMEM((nb * H_R // WKV_GROUP, gw, gw), F32)],
        compiler_params=_cparams("parallel", "arbitrary"), name="wkv_chunked",
    )(r, lw, k, v, kk, b, s0)


def _prep_weights(W):
    bf = lambda a: a.astype(BF16)
    row = lambda a: a.reshape(1, -1)
    P = dict(W)
    w_in = W["e_w_in"][0]
    n_main = 3 * D_A + 3 * D_F
    P["w_main"] = bf(w_in[:, :n_main])
    P["w_fl"] = bf(jnp.pad(w_in[:, n_main:], ((0, 0), (0, LANES - H_F))))
    P["b_f"] = jnp.pad(W["e_b_f"][0], (0, LANES - H_F)).reshape(1, LANES)
    P["w_out"] = bf(W["e_w_out"][0])
    for n in ("f_w_gate", "f_w_up", "f_w_down", "ple_gate", "ple_proj"):
        P[n] = bf(W[n])
    for n in ("r_w_r", "r_w_k", "r_w_v", "r_w_o", "r_w1", "r_w2", "r_a1", "r_a2", "r_g1", "r_g2"):
        P[n] = bf(W[n][0])
    for n in ("r_w0", "r_a0", "r_k_k", "r_k_a", "r_ln_w", "r_ln_b"):
        P[n] = row(W[n][0])
    P["r_r_k"] = W["r_r_k"][0].reshape(1, D_MODEL)
    P["r_mu"] = W["r_mu"][0]
    P["bd"] = _head_blockdiag(HD_R)
    return P


def _trunk(x, p, fox_cache, conv_prev, shift_prev, wkv_prev, P):
    B, S, D = x.shape
    n = B * S
    flat = lambda a: a.reshape(n, a.shape[-1])
    vec = lambda name, i: P[name][i].reshape(1, D)

    ya, q, k, v, kb, vx, lf, cst = _inproj(x, vec("mix_norm_pre", 0), P["w_main"], P["w_fl"], P["b_f"],
                                          conv_prev, P["e_conv_w"][0], v_transposed=fox_cache is None)
    if fox_cache is None:
        _, qc, kc = _cumsum(lf, carriers=True)
        o = _fox_prompt(q, kb, vx, qc, kc)
    else:
        ck, cv, clf = fox_cache
        Pn = ck.shape[1]
        c = _cumsum(lf, carriers=False)
        ct = jnp.swapaxes(c[:, :, :8], 1, 2)
        dt = _cumsum_lanes(jnp.swapaxes(clf, 1, 2).reshape(B * H_F, Pn)).reshape(B, H_F, Pn)
        to_t = lambda a: jnp.transpose(a, (0, 2, 3, 1))
        o = _fox_sample(q, to_t(ck), to_t(cv), dt, kb, vx, c, ct)
    tail_consts = lambda i: [vec("ffn_norm_pre", i), P["f_w_gate"], P["f_w_up"], P["f_w_down"],
                             vec("ffn_norm_post", i), vec("ple_norm", i), P["ple_gate"], P["ple_proj"]]
    p_all = p.reshape(p.shape[0], n, p.shape[-1])
    h = _tail_call(_l0_tail_kernel, 0, [flat(ya), flat(o), flat(x), p_all],
                   [P["w_out"], vec("mix_norm_post", 0)] + tail_consts(0), "l0_outproj_ffn_ple")

    consts = [vec("mix_norm_pre", 1), P["r_mu"], P["r_w_r"], P["r_w_k"], P["r_w_v"], P["r_w0"], P["r_w1"], P["r_w2"],
              P["r_a0"], P["r_a1"], P["r_a2"], P["r_g1"], P["r_g2"], P["r_k_k"], P["r_k_a"], P["bd"]]
    r, lw, km, vv, kk, bb, g, shift = _rwkv_prep(h.reshape(B, S, D), shift_prev.reshape(B, 1, D), consts)
    o1, wkv = _wkv(r, lw, km, vv, kk, bb, wkv_prev)
    h = _tail_call(_l1_tail_kernel, 1, [flat(o1), flat(r), flat(km), flat(vv), flat(g), h, p_all],
                   [P["r_ln_w"], P["r_ln_b"], P["r_r_k"], P["bd"], P["r_w_o"], vec("mix_norm_post", 1)]
                   + tail_consts(1), "l1_rwkvout_ffn_ple")

    return (h.reshape(B, S, D), k.reshape(1, B, S, H_F, HD_F), v.reshape(1, B, S, H_F, HD_F),
            lf[None, :, :, :H_F], cst[None], shift.reshape(1, B, D), wkv[None])


def kernel(x_prompt, x_sample, p_prompt, p_sample, cache_k, cache_v, cache_logf, state_conv, state_shift, state_wkv, mix_norm_pre, mix_norm_post, ffn_norm_pre, ffn_norm_post, e_w_in, e_b_f, e_conv_w, e_w_out, r_mu, r_w_r, r_w_k, r_w_v, r_w_o, r_w0, r_w1, r_w2, r_a0, r_a1, r_a2, r_g1, r_g2, r_k_k, r_k_a, r_r_k, r_ln_w, r_ln_b, f_w_gate, f_w_up, f_w_down, ple_norm, ple_gate, ple_proj):
    W = dict(mix_norm_pre=mix_norm_pre, mix_norm_post=mix_norm_post, ffn_norm_pre=ffn_norm_pre,
             ffn_norm_post=ffn_norm_post, e_w_in=e_w_in, e_b_f=e_b_f, e_conv_w=e_conv_w, e_w_out=e_w_out,
             r_mu=r_mu, r_w_r=r_w_r, r_w_k=r_w_k, r_w_v=r_w_v, r_w_o=r_w_o, r_w0=r_w0, r_w1=r_w1, r_w2=r_w2,
             r_a0=r_a0, r_a1=r_a1, r_a2=r_a2, r_g1=r_g1, r_g2=r_g2, r_k_k=r_k_k, r_k_a=r_k_a, r_r_k=r_r_k,
             r_ln_w=r_ln_w, r_ln_b=r_ln_b, f_w_gate=f_w_gate, f_w_up=f_w_up, f_w_down=f_w_down,
             ple_norm=ple_norm, ple_gate=ple_gate, ple_proj=ple_proj)
    P = _prep_weights(W)
    bp = x_prompt.shape[0]
    y_p, k_p, v_p, lf_p, c_p, sh_p, s_p = _trunk(
        x_prompt, p_prompt, None, jnp.zeros((bp, 2, D_A), F32), jnp.zeros((bp, D_MODEL), F32),
        jnp.zeros((bp, H_R, HD_R, HD_R), F32), P)
    y_s, k_s, v_s, lf_s, c_s, sh_s, s_s = _trunk(
        x_sample, p_sample, (cache_k[0], cache_v[0], cache_logf[0]), state_conv[0], state_shift[0], state_wkv[0], P)
    return (y_p, y_s, k_p, v_p, lf_p, c_p, sh_p, s_p, k_s, v_s, lf_s, c_s, sh_s, s_s)
```

```python
import functools

import jax
import jax.numpy as jnp
import numpy as np
from jax import lax
from jax.experimental import pallas as pl
from jax.experimental.pallas import tpu as pltpu

D_MODEL = 1024
D_A = 512
H_F = 8
HD_F = 64
D_F = H_F * HD_F
HD_R = 64
H_R = D_MODEL // HD_R
PLE_DIM = 256
D_FF = 2816
NORM_EPS = 1e-6
GN_EPS = 64e-5
L2_EPS = 1e-12
NEG_INF = -1e30

LOG2E = 1.4426950408889634
DECAY_SCALE = 0.6065306597126334
LANES = 128
FFN_CHUNK = 256
SAMPLE_HEADS = 8
ATTN_TILE = 1024
ATTN_Q_SUB = 256
WKV_GROUP = 2
WKV_SEQS = 4
WKV_CHUNK = 64
VMEM_LIMIT = 48 * 1024 * 1024

BF16 = jnp.bfloat16
F32 = jnp.float32


def _cparams(*sem):
    return pltpu.CompilerParams(dimension_semantics=sem, vmem_limit_bytes=VMEM_LIMIT)


def _tile(n, pref):
    t = min(n, pref)
    assert n % t == 0, (n, pref)
    return t


def _rms(x, g):
    return x * lax.rsqrt(jnp.mean(x * x, axis=-1, keepdims=True) + NORM_EPS) * g


def _bdot(a, w):
    return jnp.dot(a.astype(BF16), w, preferred_element_type=F32)


def _dot_nt(a, b, **kw):
    return lax.dot_general(a, b, (((1,), (1,)), ((), ())), preferred_element_type=F32, **kw)


def _dot_tn(a, b, **kw):
    return lax.dot_general(a, b, (((0,), (0,)), ((), ())), preferred_element_type=F32, **kw)


def _softplus(y):
    return jnp.maximum(y, 0.0) + jnp.log1p(jnp.exp(-jnp.abs(y)))


def _split3(x):
    hi = x.astype(BF16)
    r1 = x - hi.astype(F32)
    mid = r1.astype(BF16)
    lo = (r1 - mid.astype(F32)).astype(BF16)
    return hi, mid, lo


def _head_allsum(x, bd, exact=True):
    hi = x.astype(BF16)
    w = bd.shape[0]
    slabs = [jnp.dot(hi[:, t:t + w], bd, preferred_element_type=F32) for t in range(0, x.shape[1], w)]
    if exact:
        lo = (x - hi.astype(F32)).astype(BF16)
        slabs = [sb + jnp.dot(lo[:, t:t + w], bd, preferred_element_type=F32)
                 for sb, t in zip(slabs, range(0, x.shape[1], w))]
    return jnp.concatenate(slabs, axis=1)


def _head_blockdiag(hd, width=256):
    idx = jnp.arange(width) // hd
    return (idx[:, None] == idx[None, :]).astype(BF16)


def _prev_rows(first_ref, carry_ref, rows, nb, ts):
    if nb == 1:
        return carry_ref[rows]
    w = first_ref.shape[-1]
    return jnp.broadcast_to(first_ref[:, rows, :], (nb, ts, w)).reshape(nb * ts, w)


def _inproj_kernel(x_ref, g_ref, w_ref, wfl_ref, bf_ref, cprev_ref, cw_ref,
                   ya_ref, q_ref, k_ref, v_ref, kb_ref, vx_ref, lf_ref, cst_ref, carry_ref, *, v_transposed):
    nb, ts, D = x_ref.shape
    n = nb * ts
    if nb == 1:
        @pl.when(pl.program_id(1) == 0)
        def _():
            carry_ref[6:8, :] = cprev_ref[0]
        p0, p1 = carry_ref[6:7, :], carry_ref[7:8, :]
    else:
        p0 = _prev_rows(cprev_ref, None, slice(0, 1), nb, ts)
        p1 = _prev_rows(cprev_ref, None, slice(1, 2), nb, ts)

    xn = _rms(x_ref[...].reshape(n, D), g_ref[...]).astype(BF16)
    z = [jnp.dot(xn, w_ref[:, c * D_A:(c + 1) * D_A], preferred_element_type=F32) for c in range(6)]
    ax, a_b, a_c, q, k, v = z
    fl = jnp.dot(xn, wfl_ref[...], preferred_element_type=F32) + bf_ref[...]
    lf_ref[...] = (-_softplus(-fl)).reshape(nb, ts, LANES)
    u = a_c * ax
    t = lax.broadcasted_iota(jnp.int32, u.shape, 0) % ts
    um1 = jnp.where(t == 0, p1, pltpu.roll(u, 1, 0))
    um2 = jnp.where(t == 0, p0, jnp.where(t == 1, p1, pltpu.roll(u, 2, 0)))
    cu = cw_ref[0:1, :] * um2 + cw_ref[1:2, :] * um1 + cw_ref[2:3, :] * u
    ya_ref[...] = (a_b * cu).astype(BF16).reshape(nb, ts, D_A)
    u3 = u.reshape(nb, ts, D_A)
    if nb == 1:
        carry_ref[...] = u[ts - 8:ts, :]
    cst_ref[...] = u3[:, ts - 2:ts, :]
    shape = (nb, ts, D_F)
    q_ref[...] = (q * (HD_F ** -0.5 * LOG2E)).astype(BF16).reshape(shape)
    k_ref[...] = k.reshape(shape)
    v_ref[...] = v.reshape(shape)
    kb_ref[...] = k.astype(BF16).reshape(shape)
    if v_transposed:
        vx_ref[0] = v.T.astype(BF16)
    else:
        vx_ref[...] = v.astype(BF16).reshape(shape)


def _inproj(x, g, w_main, w_fl, b_f, conv_prev, conv_w, v_transposed):
    B, S, D = x.shape
    ts = _tile(S, 512)
    nb = _tile(B, 512 // ts) if ts == S and not v_transposed else 1
    tok = lambda w: pl.BlockSpec((nb, ts, w), lambda b, s: (b, s, 0))
    full = lambda a: pl.BlockSpec(a.shape, lambda b, s: (0,) * a.ndim)
    st = pl.BlockSpec((nb, 2, D_A), lambda b, s: (b, 0, 0))
    out_shape = (
        jax.ShapeDtypeStruct((B, S, D_A), BF16),
        jax.ShapeDtypeStruct((B, S, D_F), BF16),
        jax.ShapeDtypeStruct((B, S, D_F), F32),
        jax.ShapeDtypeStruct((B, S, D_F), F32),
        jax.ShapeDtypeStruct((B, S, D_F), BF16),
        jax.ShapeDtypeStruct((B, D_F, S) if v_transposed else (B, S, D_F), BF16),
        jax.ShapeDtypeStruct((B, S, LANES), F32),
        jax.ShapeDtypeStruct((B, 2, D_A), F32),
    )
    vx_spec = pl.BlockSpec((nb, D_F, ts), lambda b, s: (b, 0, s)) if v_transposed else tok(D_F)
    return pl.pallas_call(
        functools.partial(_inproj_kernel, v_transposed=v_transposed),
        grid=(B // nb, S // ts),
        in_specs=[tok(D), full(g), full(w_main), full(w_fl), full(b_f), st, full(conv_w)],
        out_specs=(tok(D_A), tok(D_F), tok(D_F), tok(D_F), tok(D_F), vx_spec, tok(LANES), st),
        out_shape=out_shape,
        scratch_shapes=[pltpu.VMEM((8, D_A), F32)],
        compiler_params=_cparams("parallel", "arbitrary"),
        name="l0_inproj_conv",
    )(x, g, w_main, w_fl, b_f, conv_prev, conv_w)


def _scan_rows(x):
    n = x.shape[0]
    row = lax.broadcasted_iota(jnp.int32, x.shape, 0)
    sh = 1
    while sh < n:
        x = x + jnp.where(row >= sh, pltpu.roll(x, sh, 0), 0.0)
        sh *= 2
    return x


def _cumsum_kernel(x_ref, o_ref):
    o_ref[...] = _scan_rows(x_ref[...]) * LOG2E


def _cumsum_lanes_kernel(x_ref, o_ref):
    x = x_ref[...]
    n = x.shape[1]
    col = lax.broadcasted_iota(jnp.int32, x.shape, 1)
    sh = 1
    while sh < n:
        x = x + jnp.where(col >= sh, pltpu.roll(x, sh, 1), 0.0)
        sh *= 2
    o_ref[...] = (x - x[:, n - 1:n]) * LOG2E


def _cumsum_lanes(x):
    spec = pl.BlockSpec(x.shape, lambda i: (0, 0))
    return pl.pallas_call(
        _cumsum_lanes_kernel, grid=(1,), in_specs=[spec], out_specs=spec,
        out_shape=jax.ShapeDtypeStruct(x.shape, F32),
        compiler_params=_cparams("arbitrary"), name="cache_logf_cumsum",
    )(x)


def _cumsum_carrier_kernel(x_ref, pq_ref, pk_ref, oq_ref, ok_ref, o_ref, qc_ref, kc_ref):
    c = _scan_rows(x_ref[...]) * LOG2E
    o_ref[...] = c
    parts = jnp.concatenate(_split3(c), axis=1)
    qc_ref[...] = (jnp.dot(parts, pq_ref[...], preferred_element_type=F32) + oq_ref[...]).astype(BF16)
    kc_ref[...] = (jnp.dot(parts, pk_ref[...], preferred_element_type=F32) + ok_ref[...]).astype(BF16)


def _carrier_lane(h, slot):
    return LANES * (h // 2) + (HD_F if h % 2 == 0 else 0) + slot


def _carrier_placement():
    pq = np.zeros((3 * LANES, D_F), np.float32)
    pk = np.zeros((3 * LANES, D_F), np.float32)
    oq = np.zeros((1, D_F), np.float32)
    ok = np.zeros((1, D_F), np.float32)
    for h in range(H_F):
        for part in range(3):
            pq[part * LANES + h, _carrier_lane(h, part)] = 1.0
            ok[0, _carrier_lane(h, part)] = 1.0
            pk[part * LANES + h, _carrier_lane(h, 3 + part)] = -1.0
            oq[0, _carrier_lane(h, 3 + part)] = 1.0
    return jnp.asarray(pq, BF16), jnp.asarray(pk, BF16), jnp.asarray(oq), jnp.asarray(ok)


def _cumsum(x, carriers):
    B, S, W = x.shape
    spec = pl.BlockSpec((None, S, W), lambda b: (b, 0, 0))
    if not carriers:
        return pl.pallas_call(
            _cumsum_kernel, grid=(B,), in_specs=[spec], out_specs=spec,
            out_shape=jax.ShapeDtypeStruct(x.shape, F32),
            compiler_params=_cparams("parallel"), name="logf_cumsum",
        )(x)
    consts = _carrier_placement()
    cspec = pl.BlockSpec((None, S, D_F), lambda b: (b, 0, 0))
    return pl.pallas_call(
        _cumsum_carrier_kernel, grid=(B,),
        in_specs=[spec] + [pl.BlockSpec(a.shape, lambda b: (0, 0)) for a in consts],
        out_specs=(spec, cspec, cspec),
        out_shape=(jax.ShapeDtypeStruct(x.shape, F32), jax.ShapeDtypeStruct((B, S, D_F), BF16),
                   jax.ShapeDtypeStruct((B, S, D_F), BF16)),
        compiler_params=_cparams("parallel"), name="logf_cumsum_carriers",
    )(x, *consts)


def _fox_prompt_kernel(q_ref, k_ref, vt_ref, qc_ref, kc_ref, o_ref, m_ref, acc_ref):
    i = pl.program_id(1)
    j = pl.program_id(2)
    tq, tk = q_ref.shape[0], k_ref.shape[0]
    tqs = min(tq, ATTN_Q_SUB)

    @pl.when(j == 0)
    def _():
        m_ref[...] = jnp.full(m_ref.shape, NEG_INF, F32)
        acc_ref[...] = jnp.zeros(acc_ref.shape, F32)

    def update(diag):
        lane_q = lax.broadcasted_iota(jnp.int32, (tq, LANES), 1)
        lane_k = lax.broadcasted_iota(jnp.int32, (tk, LANES), 1)
        row_v = lax.broadcasted_iota(jnp.int32, (LANES, tk), 0)
        for pair in range(H_F // 2):
            sl = slice(pair * LANES, (pair + 1) * LANES)
            q2, qc2, k2, kc2, vt2 = q_ref[:, sl], qc_ref[:, sl], k_ref[:, sl], kc_ref[:, sl], vt_ref[sl, :]
            own = lambda idx, e: (idx < HD_F) == (e == 0)
            qa = [jnp.where(own(lane_q, e), q2, qc2) for e in range(2)]
            ka = [jnp.where(own(lane_k, e), k2, kc2) for e in range(2)]
            va = [jnp.where(own(row_v, e), vt2, jnp.ones_like(vt2)) for e in range(2)]
            units = [(e, qs) for e in range(2) for qs in range(tq // tqs)]
            cs = [slice(qs * tqs, (qs + 1) * tqs) for _, qs in units]
            nk = [(qs + 1) * tqs if diag else tk for _, qs in units]
            st = [_dot_nt(ka[e][:nk[u]], qa[e][cs[u]]) for u, (e, _) in enumerate(units)]
            if diag:
                for u, (_, qs) in enumerate(units):
                    key = lax.broadcasted_iota(jnp.int32, (nk[u], tqs), 0)
                    qry = lax.broadcasted_iota(jnp.int32, (nk[u], tqs), 1) + qs * tqs
                    st[u] = jnp.where(key <= qry, st[u], NEG_INF)
            m_old = [m_ref[2 * pair + e, :, cs[u]] for u, (e, _) in enumerate(units)]
            m_new = [jnp.maximum(m_old[u], jnp.max(st[u], axis=0, keepdims=True)) for u in range(len(units))]
            pt = [jnp.exp2(st[u] - m_new[u]).astype(BF16) for u in range(len(units))]
            pv = [jnp.dot(va[e][:, :nk[u]], pt[u], preferred_element_type=F32) for u, (e, _) in enumerate(units)]
            for u, (e, _) in enumerate(units):
                h = 2 * pair + e
                acc_ref[h, :, cs[u]] = jnp.exp2(m_old[u] - m_new[u]) * acc_ref[h, :, cs[u]] + pv[u]
                m_ref[h, :, cs[u]] = m_new[u]

    @pl.when(j < i)
    def _():
        update(False)

    @pl.when(j == i)
    def _():
        update(True)
        for pair in range(H_F // 2):
            a, b = acc_ref[2 * pair], acc_ref[2 * pair + 1]
            ot = jnp.concatenate([a[:HD_F] / a[HD_F:HD_F + 1], b[HD_F:] / b[0:1]], axis=0)
            o_ref[:, pair * LANES:(pair + 1) * LANES] = ot.T.astype(o_ref.dtype)


def _fox_prompt(q, kb, vt, qc, kc):
    B, S, _ = q.shape
    t = _tile(S, ATTN_TILE)
    n = S // t
    qs = pl.BlockSpec((None, t, D_F), lambda b, i, j: (b, i, 0))
    ks = pl.BlockSpec((None, t, D_F), lambda b, i, j: (b, jnp.minimum(j, i), 0))
    vs = pl.BlockSpec((None, D_F, t), lambda b, i, j: (b, 0, jnp.minimum(j, i)))
    return pl.pallas_call(
        _fox_prompt_kernel, grid=(B, n, n),
        in_specs=[qs, ks, vs, qs, ks], out_specs=qs,
        out_shape=jax.ShapeDtypeStruct((B, S, D_F), BF16),
        scratch_shapes=[pltpu.VMEM((H_F, 1, t), F32), pltpu.VMEM((H_F, LANES, t), F32)],
        compiler_params=_cparams("parallel", "parallel", "arbitrary"),
        name="fox_prompt_attention",
    )(q, kb, vt, qc, kc)


def _fox_sample_kernel(q_ref, ckt_ref, cvt_ref, dt_ref, kn_ref, vn_ref, cn_ref, cnt_ref, o_ref,
                       m_ref, l_ref, acc_ref):
    j = pl.program_id(2)
    nh, T = q_ref.shape[0], q_ref.shape[1]
    heads = range(nh)

    @pl.when(j == 0)
    def _():
        m_ref[...] = jnp.full(m_ref.shape, NEG_INF, F32)
        l_ref[...] = jnp.zeros(l_ref.shape, F32)
        acc_ref[...] = jnp.zeros(acc_ref.shape, F32)

    def update(s, pv):
        m_old = [m_ref[h] for h in heads]
        m_new = [jnp.maximum(m_old[h], jnp.max(s[h], axis=-1, keepdims=True)) for h in heads]
        p = [jnp.exp2(s[h] - m_new[h]) for h in heads]
        o = [pv(h, p[h].astype(BF16)) for h in heads]
        for h in heads:
            alpha = jnp.exp2(m_old[h] - m_new[h])
            l_ref[h] = alpha * l_ref[h] + jnp.sum(p[h], axis=-1, keepdims=True)
            acc_ref[h] = alpha * acc_ref[h] + o[h]
            m_ref[h] = m_new[h]

    q = [q_ref[h] for h in heads]
    cn = [cn_ref[h] for h in heads]
    kt = [ckt_ref[h].astype(BF16) for h in heads]
    vt = [cvt_ref[h].astype(BF16) for h in heads]
    update([jnp.dot(q[h], kt[h], preferred_element_type=F32) + (cn[h] - dt_ref[h]) for h in heads],
           lambda h, p: _dot_nt(p, vt[h]))

    @pl.when(j == pl.num_programs(2) - 1)
    def _():
        causal = lax.broadcasted_iota(jnp.int32, (T, T), 0) >= lax.broadcasted_iota(jnp.int32, (T, T), 1)
        update([jnp.where(causal, _dot_nt(q[h], kn_ref[h]) + (cn[h] - cnt_ref[h]), NEG_INF) for h in heads],
               lambda h, p: jnp.dot(p, vn_ref[h], preferred_element_type=F32))
        for h in heads:
            o_ref[h] = (acc_ref[h] / l_ref[h]).astype(o_ref.dtype)


def _fox_sample(q, ckt, cvt, dt, kb, vb, cn, cnt):
    B, T, _ = q.shape
    P = ckt.shape[-1]
    tk = _tile(P, 4096)
    nh = SAMPLE_HEADS
    heads = lambda a: jnp.swapaxes(a.reshape(B, T, H_F, HD_F), 1, 2)
    new = lambda w: pl.BlockSpec((None, nh, T, w), lambda b, h, j: (b, h, 0, 0))
    cache = pl.BlockSpec((None, nh, HD_F, tk), lambda b, h, j: (b, h, 0, j))
    o = pl.pallas_call(
        _fox_sample_kernel, grid=(B, H_F // nh, P // tk),
        in_specs=[new(HD_F), cache, cache, pl.BlockSpec((None, nh, 1, tk), lambda b, h, j: (b, h, 0, j)),
                  new(HD_F), new(HD_F), new(1), pl.BlockSpec((None, nh, 1, T), lambda b, h, j: (b, h, 0, 0))],
        out_specs=new(HD_F),
        out_shape=jax.ShapeDtypeStruct((B, H_F, T, HD_F), BF16),
        scratch_shapes=[pltpu.VMEM((nh, T, 1), F32), pltpu.VMEM((nh, T, 1), F32), pltpu.VMEM((nh, T, HD_F), F32)],
        compiler_params=_cparams("parallel", "parallel", "arbitrary"),
        name="fox_sample_attention",
    )(heads(q), ckt, cvt, dt[:, :, None, :], heads(kb), heads(vb),
      jnp.swapaxes(cn[:, :, :H_F], 1, 2)[..., None], cnt[:, :, None, :])
    return jnp.swapaxes(o, 1, 2).reshape(B, T, D_F)


def _channel_tail(h, p_ref, gpre_ref, wg_ref, wu_ref, wd_ref, gpost_ref, gple_ref, wpg_ref, wpp_ref,
                  out_ref, act_ref):
    xn = _rms(h, gpre_ref[...]).astype(BF16)
    for c in range(0, D_FF, FFN_CHUNK):
        gate = jnp.dot(xn, wg_ref[:, c:c + FFN_CHUNK], preferred_element_type=F32)
        up = jnp.dot(xn, wu_ref[:, c:c + FFN_CHUNK], preferred_element_type=F32)
        act_ref[:, c:c + FFN_CHUNK] = (gate * jax.nn.sigmoid(gate) * up).astype(BF16)
    h = h + _rms(jnp.dot(act_ref[...], wd_ref[...], preferred_element_type=F32), gpost_ref[...])
    gate = jax.nn.sigmoid(_bdot(_rms(h, gple_ref[...]), wpg_ref[...]))
    out_ref[...] = h + gate * _bdot(p_ref[...], wpp_ref[...])


def _l0_tail_kernel(ya_ref, o_ref, x_ref, p_ref, wout_ref, gmix_ref, *tail):
    y = (jnp.dot(ya_ref[...], wout_ref[0:D_A, :], preferred_element_type=F32)
         + jnp.dot(o_ref[...], wout_ref[D_A:D_A + D_F, :], preferred_element_type=F32))
    _channel_tail(x_ref[...] + _rms(y, gmix_ref[...]), p_ref, *tail)


def _l1_tail_kernel(o_ref, r_ref, k_ref, v_ref, g_ref, h_ref, p_ref, lnw_ref, lnb_ref, rk_ref, bd_ref,
                    wo_ref, gmix_ref, *tail):
    bd = bd_ref[...]
    o = o_ref[...]
    d = o - _head_allsum(o, bd, exact=False) * (1.0 / HD_R)
    var = _head_allsum(d * d, bd, exact=False) * (1.0 / HD_R)
    on = d * lax.rsqrt(var + GN_EPS) * lnw_ref[...] + lnb_ref[...]
    rk = r_ref[...].astype(F32) * k_ref[...].astype(F32) * rk_ref[...]
    bonus = _head_allsum(rk, bd, exact=False) * v_ref[...].astype(F32)
    y = _bdot((on + bonus) * g_ref[...].astype(F32), wo_ref[...])
    _channel_tail(h_ref[...] + _rms(y, gmix_ref[...]), p_ref, *tail)


def _tail_call(kernel, layer, tokens, consts, name):
    n = tokens[0].shape[0]
    tm = _tile(n, 512)

    def tok(a):
        if a.ndim == 2:
            return pl.BlockSpec((tm, a.shape[1]), lambda i: (i, 0))
        return pl.BlockSpec((None, tm, a.shape[2]), lambda i: (layer, i, 0))

    def res(a):
        if a.ndim == 2:
            return pl.BlockSpec(a.shape, lambda i: (0, 0), pipeline_mode=pl.Buffered(1))
        return pl.BlockSpec((None,) + a.shape[1:], lambda i: (layer, 0, 0), pipeline_mode=pl.Buffered(1))

    return pl.pallas_call(
        kernel, grid=(n // tm,),
        in_specs=[tok(t) for t in tokens] + [res(c) for c in consts],
        out_specs=pl.BlockSpec((tm, D_MODEL), lambda i: (i, 0)),
        out_shape=jax.ShapeDtypeStruct((n, D_MODEL), F32),
        scratch_shapes=[pltpu.VMEM((tm, D_FF), BF16)],
        compiler_params=_cparams("parallel"), name=name,
    )(*tokens, *consts)


def _rwkv_prep_kernel(h_ref, sprev_ref, gpre_ref, mu_ref, wr_ref, wk_ref, wv_ref, w0_ref, w1_ref, w2_ref,
                      a0_ref, a1_ref, a2_ref, g1_ref, g2_ref, kk_ref, ka_ref, bd_ref,
                      r_out, lw_out, k_out, v_out, kk_out, b_out, g_out, shift_out, carry_ref):
    nb, ts, D = h_ref.shape
    n = nb * ts
    if nb == 1:
        @pl.when(pl.program_id(1) == 0)
        def _():
            carry_ref[7:8, :] = sprev_ref[0]
    prev = _prev_rows(sprev_ref, carry_ref, slice(7, 8) if nb == 1 else slice(0, 1), nb, ts)

    xn = _rms(h_ref[...].reshape(n, D), gpre_ref[...])
    t = lax.broadcasted_iota(jnp.int32, xn.shape, 0) % ts
    xx = jnp.where(t == 0, prev, pltpu.roll(xn, 1, 0)) - xn
    if nb == 1:
        carry_ref[...] = xn[ts - 8:ts, :]
    shift_out[...] = xn.reshape(nb, ts, D)[:, ts - 1:ts, :]
    xnb, xxb, mub = xn.astype(BF16), xx.astype(BF16), mu_ref[...].astype(BF16)
    mix = lambda n: xnb + xxb * mub[n:n + 1, :]
    r = _bdot(mix(0), wr_ref[...])
    k = _bdot(mix(2), wk_ref[...])
    v = _bdot(mix(3), wv_ref[...])
    wl = w0_ref[...] + _bdot(jnp.tanh(_bdot(mix(1), w1_ref[...])), w2_ref[...])
    a = jax.nn.sigmoid(a0_ref[...] + _bdot(_bdot(mix(4), a1_ref[...]), a2_ref[...]))
    g = _bdot(jax.nn.sigmoid(_bdot(mix(5), g1_ref[...])), g2_ref[...])
    kkr = k * kk_ref[...]
    kk = kkr * lax.rsqrt(jnp.maximum(_head_allsum(kkr * kkr, bd_ref[...]), L2_EPS * L2_EPS))
    shape = (nb, ts, D)
    r_out[...] = r.astype(BF16).reshape(shape)
    lw_out[...] = (-DECAY_SCALE * jax.nn.sigmoid(wl)).reshape(shape)
    k_out[...] = (k * (1.0 + (a - 1.0) * ka_ref[...])).astype(BF16).reshape(shape)
    v_out[...] = v.astype(BF16).reshape(shape)
    kk_out[...] = kk.astype(BF16).reshape(shape)
    b_out[...] = (kk * a).astype(BF16).reshape(shape)
    g_out[...] = g.astype(BF16).reshape(shape)


def _rwkv_prep(h, shift_prev, consts):
    B, S, D = h.shape
    ts = _tile(S, 512)
    nb = _tile(B, 512 // ts) if ts == S else 1
    tok = pl.BlockSpec((nb, ts, D), lambda b, s: (b, s, 0))
    row = pl.BlockSpec((nb, 1, D), lambda b, s: (b, 0, 0))
    full = lambda a: pl.BlockSpec(a.shape, lambda b, s: (0,) * a.ndim)
    big = lambda dt: jax.ShapeDtypeStruct((B, S, D), dt)
    return pl.pallas_call(
        _rwkv_prep_kernel, grid=(B // nb, S // ts),
        in_specs=[tok, row] + [full(c) for c in consts],
        out_specs=(tok,) * 7 + (row,),
        out_shape=(big(BF16), big(F32)) + (big(BF16),) * 5 + (jax.ShapeDtypeStruct((B, 1, D), F32),),
        scratch_shapes=[pltpu.VMEM((8, D), F32)],
        compiler_params=_cparams("parallel", "arbitrary"), name="rwkv_prep",
    )(h, shift_prev, *consts)


def _wkv_kernel(r_ref, lw_ref, k_ref, v_ref, kk_ref, b_ref, s0_ref, o_ref, s_ref, sbd_ref, *, n_chunks):
    c = pl.program_id(1)
    nb, L = r_ref.shape[0], r_ref.shape[1]
    G = WKV_GROUP
    W = G * HD_R
    assert L == HD_R and H_R % G == 0
    n_groups = H_R // G
    mm = functools.partial(jnp.dot, preferred_element_type=F32)
    units = [(bi, q) for bi in range(nb) for q in range(n_groups)]

    r1 = lax.broadcasted_iota(jnp.int32, (L, W), 0)
    blk1 = lax.broadcasted_iota(jnp.int32, (L, W), 1) // HD_R

    def blockdiag(x):
        return jnp.concatenate([jnp.where(blk1 == e, x, jnp.zeros_like(x)) for e in range(G)], axis=0)

    @pl.when(c == 0)
    def _():
        for u, (bi, q) in enumerate(units):
            rows = [jnp.concatenate([s0_ref[bi, G * q + e]] * G, axis=1) for e in range(G)]
            sbd_ref[u] = jnp.concatenate([jnp.where(blk1 == e, rows[e], 0.0) for e in range(G)], axis=0)

    at, rt, bt, kt, w_end = [], [], [], [], []
    for bi in range(nb):
        lw = lw_ref[bi]
        row = lax.broadcasted_iota(jnp.int32, lw.shape, 0)
        cum = lw
        sh = 1
        while sh < L:
            cum = cum + jnp.where(row >= sh, pltpu.roll(cum, sh, 0), 0.0)
            sh *= 2
        w_inv = jnp.exp(-cum)
        at.append((-kk_ref[bi].astype(F32) * jnp.exp(cum - lw)).astype(BF16))
        rt.append((r_ref[bi].astype(F32) * jnp.exp(cum)).astype(BF16))
        bt.append((b_ref[bi].astype(F32) * w_inv).astype(BF16))
        kt.append((k_ref[bi].astype(F32) * w_inv).astype(BF16))
        w_end.append(jnp.exp(cum[L - 1:L, :]))

    ri = lax.broadcasted_iota(jnp.int32, (2 * L, W), 0)
    ci = lax.broadcasted_iota(jnp.int32, (2 * L, W), 1)
    causal = (ri % L + ri // L) > ci % L
    blk2 = ci // HD_R
    same_head = (lax.broadcasted_iota(jnp.int32, (W, W), 0) // HD_R) == (lax.broadcasted_iota(jnp.int32, (W, W), 1) // HD_R)
    eye_g = (r1 == lax.broadcasted_iota(jnp.int32, (L, W), 1) % L).astype(F32)
    kcol = lambda e: (e + G // 2) % G
    zeros2 = jnp.zeros((2 * L, W), BF16)
    zl = jnp.zeros((L, W), BF16)

    idx = range(len(units))
    sl = [slice(q * W, (q + 1) * W) for _, q in units]
    ar = [jnp.concatenate([at[bi][:, sl[u]], rt[bi][:, sl[u]]], axis=0) for u, (bi, _) in enumerate(units)]
    btu = [bt[bi][:, sl[u]] for u, (bi, _) in enumerate(units)]
    ktu = [kt[bi][:, sl[u]] for u, (bi, _) in enumerate(units)]
    v4 = [v_ref[bi, :, sl[u]] for u, (bi, _) in enumerate(units)]
    S4 = [sbd_ref[u] for u in idx]

    def gram_rhs(u, e):
        return jnp.concatenate([btu[u] if j == e else ktu[u] if j == kcol(e) else zl for j in range(G)], axis=0)

    g = [[jnp.where(causal, _dot_nt(jnp.where(blk2 == e, ar[u], zeros2), gram_rhs(u, e)), 0.0) for e in range(G)]
         for u in idx]
    gb, gk = [], []
    for u in idx:
        b_part, k_part = g[u][0], g[u][(0 + G // 2) % G]
        for e in range(1, G):
            b_part = jnp.where(blk2 == e, g[u][e], b_part)
            k_part = jnp.where(blk2 == e, g[u][(e + G // 2) % G], k_part)
        gb.append(b_part)
        gk.append(k_part.astype(BF16))
    vx = [jnp.concatenate([jnp.where(blk1 == (j + G // 2) % G, v4[u], zl) for j in range(G)], axis=0) for u in idx]
    xs = [_dot_nt(ar[u], S4[u].astype(BF16)) + mm(gk[u], vx[u]) for u in idx]
    inv = [eye_g + x[:L] for x in gb]
    pw = [x[:L].astype(BF16) for x in gb]
    pw = [mm(x, blockdiag(x)) for x in pw]
    sq = 2
    while sq < L:
        pwb = [x.astype(BF16) for x in pw]
        both = [mm(jnp.concatenate([inv[u].astype(BF16), pwb[u]], axis=0), blockdiag(pwb[u])) for u in idx]
        inv = [inv[u] + both[u][:L] for u in idx]
        pw = [x[L:] for x in both]
        sq *= 2
    ub = [mm(inv[u].astype(BF16), blockdiag(xs[u][:L].astype(BF16))).astype(BF16) for u in idx]
    for u, (bi, _) in enumerate(units):
        o_ref[bi, :, sl[u]] = xs[u][L:] + mm(gb[u][L:].astype(BF16), blockdiag(ub[u]))
    s_new = []
    for u, (bi, _) in enumerate(units):
        upd = _dot_tn(jnp.concatenate([ub[u], v4[u]], axis=0), jnp.concatenate([btu[u], ktu[u]], axis=0))
        s_new.append((S4[u] + jnp.where(same_head, upd, 0.0)) * w_end[bi][:, sl[u]])
        sbd_ref[u] = s_new[u]

    def emit_state():
        for u, (bi, q) in enumerate(units):
            for e in range(G):
                s_ref[bi, G * q + e] = s_new[u][e * HD_R:(e + 1) * HD_R, e * HD_R:(e + 1) * HD_R]

    if n_chunks == 1:
        emit_state()
    else:
        pl.when(c == n_chunks - 1)(emit_state)


def _wkv(r, lw, k, v, kk, b, s0):
    B, S, D = r.shape
    L = _tile(S, WKV_CHUNK)
    nb = _tile(B, WKV_SEQS)
    gw = WKV_GROUP * HD_R
    tok = pl.BlockSpec((nb, L, D), lambda b_, c: (b_, c, 0))
    st = pl.BlockSpec((nb, H_R, HD_R, HD_R), lambda b_, c: (b_, 0, 0, 0))
    return pl.pallas_call(
        functools.partial(_wkv_kernel, n_chunks=S // L), grid=(B // nb, S // L),
        in_specs=[tok] * 6 + [st], out_specs=(tok, st),
        out_shape=(jax.ShapeDtypeStruct((B, S, D), F32), jax.ShapeDtypeStruct(s0.shape, F32)),
        scratch_shapes=[pltpu.VMEM((nb * H_R // WKV_GROUP, gw, gw), F32)],
        compiler_params=_cparams("parallel", "arbitrary"), name="wkv_chunked",
    )(r, lw, k, v, kk, b, s0)


def _prep_weights(W):
    bf = lambda a: a.astype(BF16)
    row = lambda a: a.reshape(1, -1)
    P = dict(W)
    w_in = W["e_w_in"][0]
    n_main = 3 * D_A + 3 * D_F
    P["w_main"] = bf(w_in[:, :n_main])
    P["w_fl"] = bf(jnp.pad(w_in[:, n_main:], ((0, 0), (0, LANES - H_F))))
    P["b_f"] = jnp.pad(W["e_b_f"][0], (0, LANES - H_F)).reshape(1, LANES)
    P["w_out"] = bf(W["e_w_out"][0])
    for n in ("f_w_gate", "f_w_up", "f_w_down", "ple_gate", "ple_proj"):
        P[n] = bf(W[n])
    for n in ("r_w_r", "r_w_k", "r_w_v", "r_w_o", "r_w1", "r_w2", "r_a1", "r_a2", "r_g1", "r_g2"):
        P[n] = bf(W[n][0])
    for n in ("r_w0", "r_a0", "r_k_k", "r_k_a", "r_ln_w", "r_ln_b"):
        P[n] = row(W[n][0])
    P["r_r_k"] = W["r_r_k"][0].reshape(1, D_MODEL)
    P["r_mu"] = W["r_mu"][0]
    P["bd"] = _head_blockdiag(HD_R)
    return P


def _trunk(x, p, fox_cache, conv_prev, shift_prev, wkv_prev, P):
    B, S, D = x.shape
    n = B * S
    flat = lambda a: a.reshape(n, a.shape[-1])
    vec = lambda name, i: P[name][i].reshape(1, D)

    ya, q, k, v, kb, vx, lf, cst = _inproj(x, vec("mix_norm_pre", 0), P["w_main"], P["w_fl"], P["b_f"],
                                          conv_prev, P["e_conv_w"][0], v_transposed=fox_cache is None)
    if fox_cache is None:
        _, qc, kc = _cumsum(lf, carriers=True)
        o = _fox_prompt(q, kb, vx, qc, kc)
    else:
        ck, cv, clf = fox_cache
        Pn = ck.shape[1]
        c = _cumsum(lf, carriers=False)
        ct = jnp.swapaxes(c[:, :, :8], 1, 2)
        dt = _cumsum_lanes(jnp.swapaxes(clf, 1, 2).reshape(B * H_F, Pn)).reshape(B, H_F, Pn)
        to_t = lambda a: jnp.transpose(a, (0, 2, 3, 1))
        o = _fox_sample(q, to_t(ck), to_t(cv), dt, kb, vx, c, ct)
    tail_consts = lambda i: [vec("ffn_norm_pre", i), P["f_w_gate"], P["f_w_up"], P["f_w_down"],
                             vec("ffn_norm_post", i), vec("ple_norm", i), P["ple_gate"], P["ple_proj"]]
    p_all = p.reshape(p.shape[0], n, p.shape[-1])
    h = _tail_call(_l0_tail_kernel, 0, [flat(ya), flat(o), flat(x), p_all],
                   [P["w_out"], vec("mix_norm_post", 0)] + tail_consts(0), "l0_outproj_ffn_ple")

    consts = [vec("mix_norm_pre", 1), P["r_mu"], P["r_w_r"], P["r_w_k"], P["r_w_v"], P["r_w0"], P["r_w1"], P["r_w2"],
              P["r_a0"], P["r_a1"], P["r_a2"], P["r_g1"], P["r_g2"], P["r_k_k"], P["r_k_a"], P["bd"]]
    r, lw, km, vv, kk, bb, g, shift = _rwkv_prep(h.reshape(B, S, D), shift_prev.reshape(B, 1, D), consts)
    o1, wkv = _wkv(r, lw, km, vv, kk, bb, wkv_prev)
    h = _tail_call(_l1_tail_kernel, 1, [flat(o1), flat(r), flat(km), flat(vv), flat(g), h, p_all],
                   [P["r_ln_w"], P["r_ln_b"], P["r_r_k"], P["bd"], P["r_w_o"], vec("mix_norm_post", 1)]
                   + tail_consts(1), "l1_rwkvout_ffn_ple")

    return (h.reshape(B, S, D), k.reshape(1, B, S, H_F, HD_F), v.reshape(1, B, S, H_F, HD_F),
            lf[None, :, :, :H_F], cst[None], shift.reshape(1, B, D), wkv[None])


def kernel(x_prompt, x_sample, p_prompt, p_sample, cache_k, cache_v, cache_logf, state_conv, state_shift, state_wkv, mix_norm_pre, mix_norm_post, ffn_norm_pre, ffn_norm_post, e_w_in, e_b_f, e_conv_w, e_w_out, r_mu, r_w_r, r_w_k, r_w_v, r_w_o, r_w0, r_w1, r_w2, r_a0, r_a1, r_a2, r_g1, r_g2, r_k_k, r_k_a, r_r_k, r_ln_w, r_ln_b, f_w_gate, f_w_up, f_w_down, ple_norm, ple_gate, ple_proj):
    W = dict(mix_norm_pre=mix_norm_pre, mix_norm_post=mix_norm_post, ffn_norm_pre=ffn_norm_pre,
             ffn_norm_post=ffn_norm_post, e_w_in=e_w_in, e_b_f=e_b_f, e_conv_w=e_conv_w, e_w_out=e_w_out,
             r_mu=r_mu, r_w_r=r_w_r, r_w_k=r_w_k, r_w_v=r_w_v, r_w_o=r_w_o, r_w0=r_w0, r_w1=r_w1, r_w2=r_w2,
             r_a0=r_a0, r_a1=r_a1, r_a2=r_a2, r_g1=r_g1, r_g2=r_g2, r_k_k=r_k_k, r_k_a=r_k_a, r_r_k=r_r_k,
             r_ln_w=r_ln_w, r_ln_b=r_ln_b, f_w_gate=f_w_gate, f_w_up=f_w_up, f_w_down=f_w_down,
             ple_norm=ple_norm, ple_gate=ple_gate, ple_proj=ple_proj)
    P = _prep_weights(W)
    bp = x_prompt.shape[0]
    y_p, k_p, v_p, lf_p, c_p, sh_p, s_p = _trunk(
        x_prompt, p_prompt, None, jnp.zeros((bp, 2, D_A), F32), jnp.zeros((bp, D_MODEL), F32),
        jnp.zeros((bp, H_R, HD_R, HD_R), F32), P)
    y_s, k_s, v_s, lf_s, c_s, sh_s, s_s = _trunk(
        x_sample, p_sample, (cache_k[0], cache_v[0], cache_logf[0]), state_conv[0], state_shift[0], state_wkv[0], P)
    return (y_p, y_s, k_p, v_p, lf_p, c_p, sh_p, s_p, k_s, v_s, lf_s, c_s, sh_s, s_s)
```
